```python
import math
import jax, jax.numpy as jnp
from jax import lax
import numpy as np

D_MODEL = 1024
BATCH = 32
SEQ = 256
DEPTH = 4
DEC_BATCH = 8
DEC_SEQ = 1024
PAST_LEN = 512

GRID_W = 64
N_MIXERS = 2
N_HYENA_LAYERS = (DEPTH + 1) // 2
N_ATTN_LAYERS = DEPTH // 2
N_HEADS = 8
N_KV_HEADS = 2
HEAD_DIM = 128
QKV_DIM = (N_HEADS + 2 * N_KV_HEADS) * HEAD_DIM
Q_BLOCK = 128
ROPE_THETA = 10000.0
QK_EPS = 1e-6
HYENA_ORDER = 2
POS_EMB_DIM = 33
FILTER_WIDTH = 64
N_INNER_MLPS = 2
FAST_DECAY_PCT = 0.3
SLOW_DECAY_PCT = 1.5
DECAY_TARGET = 1e-2
MOD_SHIFT = 0.0
D_FF = 2816
LN_EPS = 1e-5
N_MOD = 6
DN_ALPHA = (2 * DEPTH) ** 0.25
DN_BETA = (8 * DEPTH) ** -0.25

kernel_name = 'hybrid_hyena_gqa_diffusion_step'

F32 = jnp.float32


def _layer_norm(x, g, b):
    xf = x.astype(F32)
    mu = jnp.mean(xf, -1, keepdims=True)
    var = jnp.mean(jnp.square(xf - mu), -1, keepdims=True)
    return ((xf - mu) * lax.rsqrt(var + LN_EPS) * g + b).astype(x.dtype)


def _rms_norm(x, g):
    xf = x.astype(F32)
    return (xf * lax.rsqrt(jnp.mean(jnp.square(xf), -1, keepdims=True) + QK_EPS) * g).astype(x.dtype)


def _dwconv3(x, w, b):
    L = x.shape[1]
    xp = jnp.pad(x, ((0, 0), (1, 1), (0, 0)))
    return xp[:, :L] * w[0] + xp[:, 1:L + 1] * w[1] + xp[:, 2:L + 2] * w[2] + b


def _modulation(cond, w_mod, b_mod):
    m = jax.nn.silu(cond) @ w_mod + b_mod
    return m.reshape(cond.shape[0], 1, N_MOD, D_MODEL)


def _hyena_filters(L, w1, b1, w2, b2, w_out, freq):
    t = jnp.linspace(0.0, 1.0, L, dtype=F32)[:, None]
    n_bands = (POS_EMB_DIM - 1) // 2
    w = 2.0 * math.pi * jnp.arange(L, dtype=F32) / L
    f = jnp.linspace(1e-4, n_bands - 1, n_bands, dtype=F32)
    ang = w[:, None] * f[None, :]
    z = jnp.concatenate([t, jnp.cos(ang), -jnp.sin(ang)], -1)
    h = jnp.sin(freq * (z @ w1 + b1))
    for i in range(N_INNER_MLPS):
        h = jnp.sin(freq * (h @ w2[i] + b2[i]))
    k = (h @ w_out).astype(F32).reshape(L, 2, D_MODEL)
    max_decay = math.log(DECAY_TARGET) / FAST_DECAY_PCT
    min_decay = math.log(DECAY_TARGET) / SLOW_DECAY_PCT
    deltas = jnp.abs(jnp.linspace(min_decay, max_decay, D_MODEL, dtype=F32))
    decay = jnp.exp(-t * deltas)
    return k * (decay + MOD_SHIFT)[:, None, :]


def _bidir_fftconv(u, k, bias):
    L = u.shape[1]
    k_full = jnp.concatenate([k[:, 0], jnp.zeros((1, D_MODEL), F32), k[:0:-1, 1]], 0)
    uf = u.astype(F32)
    y = jnp.fft.irfft(jnp.fft.rfft(uf, n=2 * L, axis=1) * jnp.fft.rfft(k_full, axis=0)[None],
                      n=2 * L, axis=1)[:, :L]
    return (y + uf * bias.astype(F32)).astype(u.dtype)


def _hyena_mixer(h, in_w, in_b, short_w, short_b, pw1, pb1, pw2, pb2, pwout, freq, filt_bias, out_w, out_b):
    L = h.shape[1]
    z = _dwconv3(h @ in_w + in_b, short_w, short_b)
    x0, x1, v = jnp.split(z, HYENA_ORDER + 1, axis=-1)
    k = _hyena_filters(L, pw1, pb1, pw2, pb2, pwout, freq)
    v = _bidir_fftconv(v * x1, k, filt_bias)
    return (x0 * v) @ out_w + out_b


def _qkv_heads(h, w_qkv, b_qkv, q_gain, k_gain):
    B, L, _ = h.shape
    z = h @ w_qkv + b_qkv
    q, k, v = jnp.split(z, [N_HEADS * HEAD_DIM, (N_HEADS + N_KV_HEADS) * HEAD_DIM], axis=-1)
    q = _rms_norm(q.reshape(B, L, N_HEADS, HEAD_DIM), q_gain)
    k = _rms_norm(k.reshape(B, L, N_KV_HEADS, HEAD_DIM), k_gain)
    return q, k, v.reshape(B, L, N_KV_HEADS, HEAD_DIM)


def _axial_rope_tables(L):
    n_rows = L // GRID_W
    rows = jnp.repeat(jnp.arange(n_rows), GRID_W).astype(F32)
    cols = jnp.tile(jnp.arange(GRID_W), n_rows).astype(F32)
    half = HEAD_DIM // 2
    inv = ROPE_THETA ** (-jnp.arange(0, half, 2, dtype=F32) / half)
    ang = jnp.concatenate([rows[:, None] * inv, cols[:, None] * inv], -1)
    return jnp.cos(ang), jnp.sin(ang)


def _apply_rope(x, cos, sin):
    xf = x.astype(F32).reshape(x.shape[:-1] + (HEAD_DIM // 2, 2))
    x1, x2 = xf[..., 0], xf[..., 1]
    c = cos[None, :, None, :]
    s = sin[None, :, None, :]
    return jnp.stack([x1 * c - x2 * s, x1 * s + x2 * c], -1).reshape(x.shape).astype(x.dtype)


def _blocked_gqa(q, k, v):
    B, Lq = q.shape[:2]
    G = N_HEADS // N_KV_HEADS
    nb = Lq // Q_BLOCK
    qb = q.reshape(B, nb, Q_BLOCK, N_KV_HEADS, G, HEAD_DIM).transpose(1, 0, 2, 3, 4, 5)
    kf = k.astype(F32)
    vf = v.astype(F32)
    scale = HEAD_DIM ** -0.5

    def one_block(qblk):
        s = jnp.einsum('bqkgd,bskd->bkgqs', qblk.astype(F32), kf) * scale
        p = jax.nn.softmax(s, axis=-1)
        return jnp.einsum('bkgqs,bskd->bqkgd', p, vf).astype(q.dtype)

    o = lax.map(one_block, qb)
    return o.transpose(1, 0, 2, 3, 4, 5).reshape(B, Lq, N_HEADS * HEAD_DIM)


def _attn_context(h, w_qkv, b_qkv, q_gain, k_gain, w_o, b_o):
    q, k, v = _qkv_heads(h, w_qkv, b_qkv, q_gain, k_gain)
    return _blocked_gqa(q, k, v) @ w_o + b_o, k, v


def _attn_latent(h, k_ctx, v_ctx, w_qkv, b_qkv, q_gain, k_gain, w_o, b_o):
    q, k, v = _qkv_heads(h, w_qkv, b_qkv, q_gain, k_gain)
    cos, sin = _axial_rope_tables(h.shape[1])
    q = _apply_rope(q, cos, sin)
    k = _apply_rope(k, cos, sin)
    k_all = jnp.concatenate([k, k_ctx.astype(k.dtype)], axis=1)
    v_all = jnp.concatenate([v, v_ctx.astype(v.dtype)], axis=1)
    return _blocked_gqa(q, k_all, v_all) @ w_o + b_o


def _conv_ffn(h, w_in, b_in, conv_w, conv_b, w_out, b_out):
    u = _dwconv3(h @ w_in + b_in, conv_w, conv_b)
    g, val = jnp.split(u, 2, axis=-1)
    return (jax.nn.gelu(g) * val) @ w_out + b_out


def _trunk(x, cond, P, cache_k=None, cache_v=None):
    is_ctx = cache_k is None
    new_k, new_v = [], []
    for l in range(DEPTH):
        mods = _modulation(cond, P['w_mod'][l], P['b_mod'][l])
        j = l // N_MIXERS
        h = x * (1.0 + mods[:, :, 1]) + mods[:, :, 0]
        if l % N_MIXERS == 0:
            out = _hyena_mixer(h, P['hy_in_w'][j], P['hy_in_b'][j], P['hy_short_w'][j], P['hy_short_b'][j],
                               P['hy_pos_w1'][j], P['hy_pos_b1'][j], P['hy_pos_w2'][j], P['hy_pos_b2'][j],
                               P['hy_pos_wout'][j], P['hy_freq'][j], P['hy_filt_bias'][j],
                               P['hy_out_w'][j], P['hy_out_b'][j])
        elif is_ctx:
            out, k, v = _attn_context(h, P['at_qkv_w'][j], P['at_qkv_b'][j], P['at_q_gain'][j],
                                      P['at_k_gain'][j], P['at_o_w'][j], P['at_o_b'][j])
            new_k.append(k)
            new_v.append(v)
        else:
            out = _attn_latent(h, cache_k[:, j], cache_v[:, j], P['at_qkv_w'][j], P['at_qkv_b'][j],
                               P['at_q_gain'][j], P['at_k_gain'][j], P['at_o_w'][j], P['at_o_b'][j])
        x = _layer_norm(DN_ALPHA * x + (1.0 + mods[:, :, 2]) * out, P['ln_g'][l, 0], P['ln_b'][l, 0])
        h = x * (1.0 + mods[:, :, 4]) + mods[:, :, 3]
        out = _conv_ffn(h, P['ff_in_w'][l], P['ff_in_b'][l], P['ff_conv_w'][l], P['ff_conv_b'][l],
                        P['ff_out_w'][l], P['ff_out_b'][l])
        x = _layer_norm(DN_ALPHA * x + (1.0 + mods[:, :, 5]) * out, P['ln_g'][l, 1], P['ln_b'][l, 1])
    if is_ctx:
        return x, jnp.stack(new_k, axis=1), jnp.stack(new_v, axis=1)
    return x


def setup_inputs(seed: int = 0) -> dict:
    key = jax.random.key(seed)
    ks = iter(jax.random.split(key, 48))

    def nrm(shape, std):
        return std * jax.random.normal(next(ks), shape, F32)

    D = D_MODEL
    NH, NA = N_HYENA_LAYERS, N_ATTN_LAYERS
    qk_w = nrm((NA, D, (N_HEADS + N_KV_HEADS) * HEAD_DIM), D ** -0.5)
    v_w = nrm((NA, D, N_KV_HEADS * HEAD_DIM), DN_BETA * D ** -0.5)
    return {
        'x_prompt': nrm((BATCH, SEQ, D), 1.0),
        'x_sample': nrm((DEC_BATCH, DEC_SEQ, D), 1.0),
        'cache_k': nrm((DEC_BATCH, NA, PAST_LEN, N_KV_HEADS, HEAD_DIM), 1.0),
        'cache_v': nrm((DEC_BATCH, NA, PAST_LEN, N_KV_HEADS, HEAD_DIM), 0.5),
        'c': nrm((DEC_BATCH, D), 1.0),
        'c_ctx': nrm((D,), 1.0),
        'w_mod': nrm((DEPTH, D, N_MOD * D), 0.2 * D ** -0.5),
        'b_mod': nrm((DEPTH, N_MOD * D), 0.01),
        'ln_g': 1.0 + nrm((DEPTH, 2, D), 0.01),
        'ln_b': nrm((DEPTH, 2, D), 0.01),
        'hy_in_w': nrm((NH, D, (HYENA_ORDER + 1) * D), D ** -0.5),
        'hy_in_b': nrm((NH, (HYENA_ORDER + 1) * D), 0.01),
        'hy_short_w': nrm((NH, 3, (HYENA_ORDER + 1) * D), 3 ** -0.5),
        'hy_short_b': nrm((NH, (HYENA_ORDER + 1) * D), 0.01),
        'hy_pos_w1': nrm((NH, POS_EMB_DIM, FILTER_WIDTH), POS_EMB_DIM ** -0.5),
        'hy_pos_b1': nrm((NH, FILTER_WIDTH), 0.01),
        'hy_pos_w2': nrm((NH, N_INNER_MLPS, FILTER_WIDTH, FILTER_WIDTH), FILTER_WIDTH ** -0.5),
        'hy_pos_b2': nrm((NH, N_INNER_MLPS, FILTER_WIDTH), 0.01),
        'hy_pos_wout': nrm((NH, FILTER_WIDTH, 2 * (HYENA_ORDER - 1) * D), 0.1 * FILTER_WIDTH ** -0.5),
        'hy_freq': 1.0 + nrm((NH, FILTER_WIDTH), 0.01),
        'hy_filt_bias': nrm((NH, D), 0.1),
        'hy_out_w': nrm((NH, D, D), DN_BETA * D ** -0.5),
        'hy_out_b': nrm((NH, D), 0.01),
        'at_qkv_w': jnp.concatenate([qk_w, v_w], axis=-1),
        'at_qkv_b': nrm((NA, QKV_DIM), 0.01),
        'at_q_gain': 1.0 + nrm((NA, HEAD_DIM), 0.01),
        'at_k_gain': 1.0 + nrm((NA, HEAD_DIM), 0.01),
        'at_o_w': nrm((NA, N_HEADS * HEAD_DIM, D), DN_BETA * (N_HEADS * HEAD_DIM) ** -0.5),
        'at_o_b': nrm((NA, D), 0.01),
        'ff_in_w': nrm((DEPTH, D, 2 * D_FF), DN_BETA * D ** -0.5),
        'ff_in_b': nrm((DEPTH, 2 * D_FF), 0.01),
        'ff_conv_w': nrm((DEPTH, 3, 2 * D_FF), 3 ** -0.5),
        'ff_conv_b': nrm((DEPTH, 2 * D_FF), 0.01),
        'ff_out_w': nrm((DEPTH, D_FF, D), DN_BETA * D_FF ** -0.5),
        'ff_out_b': nrm((DEPTH, D), 0.01),
    }


def reference(x_prompt, x_sample, cache_k, cache_v, c, c_ctx, w_mod, b_mod, ln_g, ln_b,
              hy_in_w, hy_in_b, hy_short_w, hy_short_b, hy_pos_w1, hy_pos_b1, hy_pos_w2, hy_pos_b2,
              hy_pos_wout, hy_freq, hy_filt_bias, hy_out_w, hy_out_b,
              at_qkv_w, at_qkv_b, at_q_gain, at_k_gain, at_o_w, at_o_b,
              ff_in_w, ff_in_b, ff_conv_w, ff_conv_b, ff_out_w, ff_out_b):
    P = {
        'w_mod': w_mod, 'b_mod': b_mod, 'ln_g': ln_g, 'ln_b': ln_b,
        'hy_in_w': hy_in_w, 'hy_in_b': hy_in_b, 'hy_short_w': hy_short_w, 'hy_short_b': hy_short_b,
        'hy_pos_w1': hy_pos_w1, 'hy_pos_b1': hy_pos_b1, 'hy_pos_w2': hy_pos_w2, 'hy_pos_b2': hy_pos_b2,
        'hy_pos_wout': hy_pos_wout, 'hy_freq': hy_freq, 'hy_filt_bias': hy_filt_bias,
        'hy_out_w': hy_out_w, 'hy_out_b': hy_out_b,
        'at_qkv_w': at_qkv_w, 'at_qkv_b': at_qkv_b, 'at_q_gain': at_q_gain, 'at_k_gain': at_k_gain,
        'at_o_w': at_o_w, 'at_o_b': at_o_b,
        'ff_in_w': ff_in_w, 'ff_in_b': ff_in_b, 'ff_conv_w': ff_conv_w, 'ff_conv_b': ff_conv_b,
        'ff_out_w': ff_out_w, 'ff_out_b': ff_out_b,
    }
    y_prompt, new_cache_k, new_cache_v = _trunk(x_prompt, c_ctx[None, :], P)
    y_sample = _trunk(x_sample, c, P, cache_k, cache_v)
    return (y_prompt, y_sample, new_cache_k, new_cache_v)
```

```python
import functools
import math

import jax
import jax.numpy as jnp
import numpy as np
from jax import lax
from jax.experimental import pallas as pl
from jax.experimental.pallas import tpu as pltpu

D_MODEL = 1024
DEPTH = 4
GRID_W = 64
N_HEADS = 8
N_KV_HEADS = 2
HEAD_DIM = 128
GROUP = N_HEADS // N_KV_HEADS
Q_DIM = N_HEADS * HEAD_DIM
KV_DIM = N_KV_HEADS * HEAD_DIM
QKV_DIM = Q_DIM + 2 * KV_DIM
ROPE_THETA = 10000.0
QK_EPS = 1e-6
POS_EMB_DIM = 33
FILTER_WIDTH = 64
N_INNER_MLPS = 2
FAST_DECAY_PCT = 0.3
SLOW_DECAY_PCT = 1.5
DECAY_TARGET = 1e-2
MOD_SHIFT = 0.0
D_FF = 2816
LN_EPS = 1e-5
N_MOD = 6
DN_ALPHA = (2 * DEPTH) ** 0.25

F32 = jnp.float32
BF16 = jnp.bfloat16

LANES = 128
MXU_WIDTH = 256
ROW_TILE = 1024
FF_CHUNK = MXU_WIDTH
HY_CHUNK = MXU_WIDTH
MOD_COL_TILE = 1536
COND_ROWS = 16
FILTER_PAD = LANES
VMEM_LIMIT = 56 * 1024 * 1024


def _params(n_axes):
    return pltpu.CompilerParams(dimension_semantics=("arbitrary",) * n_axes,
                                vmem_limit_bytes=VMEM_LIMIT)


def _resident(shape):
    nd = len(shape)
    return pl.BlockSpec(shape, lambda *_: (0,) * nd, pipeline_mode=pl.Buffered(1))


def _mods_spec(per_batch):
    if per_batch:
        return pl.BlockSpec((1, N_MOD, D_MODEL), lambda i: (i + 1, 0, 0))
    return pl.BlockSpec((1, N_MOD, D_MODEL), lambda i: (0, 0, 0))


def _modulate(x, m, shift_idx):
    return x * (1.0 + m[shift_idx + 1:shift_idx + 2]) + m[shift_idx:shift_idx + 1]


def _residual_layer_norm(x, out, gate, g, b):
    y = DN_ALPHA * x + (1.0 + gate) * out
    mu = jnp.mean(y, axis=-1, keepdims=True)
    yc = y - mu
    var = jnp.mean(yc * yc, axis=-1, keepdims=True)
    return yc * lax.rsqrt(var + LN_EPS) * g + b


def _dwconv3(z, w, b, first, last):
    n = z.shape[0]
    zp = jnp.where(first, 0.0, pltpu.roll(z, 1, 0))
    zn = jnp.where(last, 0.0, pltpu.roll(z, n - 1, 0))
    return zp * w[0:1] + z * w[1:2] + zn * w[2:3] + b


def _seq_bounds(n_rows, seq_len):
    pos = lax.broadcasted_iota(jnp.int32, (n_rows, 1), 0) & (seq_len - 1)
    return pos == 0, pos == seq_len - 1


def _mod_kernel(cond_ref, w_ref, b_ref, o_ref):
    c = cond_ref[...]
    s = (c * jax.nn.sigmoid(c)).astype(BF16)
    o_ref[0] = jnp.dot(s, w_ref[0].astype(BF16), preferred_element_type=F32) + b_ref[0]


def _modulation_all(cond, w_mod, b_mod):
    n_col = (N_MOD * D_MODEL) // MOD_COL_TILE
    return pl.pallas_call(
        _mod_kernel,
        grid=(DEPTH, n_col),
        in_specs=[pl.BlockSpec((COND_ROWS, D_MODEL), lambda l, j: (0, 0)),
                  pl.BlockSpec((1, D_MODEL, MOD_COL_TILE), lambda l, j: (l, 0, j)),
                  pl.BlockSpec((1, 1, MOD_COL_TILE), lambda l, j: (l, 0, j))],
        out_specs=pl.BlockSpec((1, COND_ROWS, MOD_COL_TILE), lambda l, j: (l, 0, j)),
        out_shape=jax.ShapeDtypeStruct((DEPTH, COND_ROWS, N_MOD * D_MODEL), F32),
        compiler_params=_params(2),
        name="modulation",
    )(cond, w_mod, b_mod.reshape(DEPTH, 1, N_MOD * D_MODEL))


def _hyena_in_kernel(seq_len, x_ref, mod_ref, w_ref, b_ref, sw_ref, sb_ref, x0_ref, u_ref):
    m = mod_ref[0]
    h = _modulate(x_ref[...], m, 0).astype(BF16)
    first, last = _seq_bounds(h.shape[0], seq_len)
    c_w = HY_CHUNK
    for c in range(D_MODEL // c_w):
        z = jnp.dot(h, w_ref[c], preferred_element_type=F32) + b_ref[c]
        zc = _dwconv3(z, sw_ref[c], sb_ref[c], first, last)
        x0_ref[:, c * c_w:(c + 1) * c_w] = zc[:, :c_w].astype(BF16)
        u_ref[:, c * c_w:(c + 1) * c_w] = (zc[:, 2 * c_w:] * zc[:, c_w:2 * c_w]).astype(BF16)


def _hyena_in(x, mods, w, b, sw, sb, seq_len, per_batch):
    t = x.shape[0]
    n_c = D_MODEL // HY_CHUNK
    row = pl.BlockSpec((ROW_TILE, D_MODEL), lambda i: (i, 0))
    return pl.pallas_call(
        functools.partial(_hyena_in_kernel, seq_len),
        grid=(t // ROW_TILE,),
        in_specs=[row, _mods_spec(per_batch),
                  _resident((n_c, D_MODEL, 3 * HY_CHUNK)), _resident((n_c, 1, 3 * HY_CHUNK)),
                  _resident((n_c, 3, 3 * HY_CHUNK)), _resident((n_c, 1, 3 * HY_CHUNK))],
        out_specs=[row, row],
        out_shape=[jax.ShapeDtypeStruct((t, D_MODEL), BF16)] * 2,
        compiler_params=_params(1),
        name="hyena_in",
    )(x, mods, w, b, sw, sb)


def _hyena_filter_kernel(seq_len, z_ref, w1_ref, b1_ref, w2_ref, b2_ref, freq_ref, wo0_ref, wo1_ref,
                         t_ref, delta_ref, fm_ref, ha_ref, hb_ref, hc_ref):
    hi = lax.Precision.HIGHEST
    freq = freq_ref[...]
    h = jnp.sin(freq * (jnp.dot(z_ref[...], w1_ref[...], precision=hi, preferred_element_type=F32)
                        + b1_ref[...]))
    for i in range(N_INNER_MLPS):
        h = jnp.sin(freq * (jnp.dot(h, w2_ref[i], precision=hi, preferred_element_type=F32) + b2_ref[i]))
    decay = jnp.exp(-t_ref[...] * delta_ref[...]) + MOD_SHIFT
    k0 = jnp.dot(h, wo0_ref[...], precision=hi, preferred_element_type=F32) * decay
    k1 = jnp.dot(h, wo1_ref[...], precision=hi, preferred_element_type=F32) * decay
    row0 = lax.broadcasted_iota(jnp.int32, (seq_len, 1), 0) == 0
    k1 = jnp.where(row0, 0.0, k1)
    fm = fm_ref[...]
    s0 = jnp.dot(fm, k0.astype(BF16), preferred_element_type=F32)
    s1 = jnp.dot(fm, k1.astype(BF16), preferred_element_type=F32)
    ha = s0[:seq_len] + s1[:seq_len]
    s0b, s1b = s0[seq_len:], s1[seq_len:]
    nyq = s0b + s1b
    ha_ref[...] = ha
    hb_ref[...] = jnp.where(row0, 0.0, s0b - s1b)
    hc_ref[...] = jnp.where(row0, nyq, ha)


def _hyena_filter(feats, w1, b1, w2, b2, freq, wout, t_col, deltas, fm, seq_len):
    dt = MXU_WIDTH
    n_d = D_MODEL // dt
    out = pl.BlockSpec((seq_len, dt), lambda j: (0, j))
    return pl.pallas_call(
        functools.partial(_hyena_filter_kernel, seq_len),
        grid=(n_d,),
        in_specs=[_resident((seq_len, FILTER_PAD)), _resident((FILTER_PAD, FILTER_PAD)),
                  _resident((1, FILTER_PAD)), _resident((N_INNER_MLPS, FILTER_PAD, FILTER_PAD)),
                  _resident((N_INNER_MLPS, 1, FILTER_PAD)), _resident((1, FILTER_PAD)),
                  pl.BlockSpec((FILTER_PAD, dt), lambda j: (0, j)),
                  pl.BlockSpec((FILTER_PAD, dt), lambda j: (0, n_d + j)),
                  _resident((seq_len, 1)), pl.BlockSpec((1, dt), lambda j: (0, j)),
                  _resident((2 * seq_len, seq_len))],
        out_specs=[out, out, out],
        out_shape=[jax.ShapeDtypeStruct((seq_len, D_MODEL), F32)] * 3,
        compiler_params=_params(1),
        name="hyena_filter",
    )(feats, w1, b1, w2, b2, freq, wout, wout, t_col, deltas, fm)


def _dft_conv_kernel(seq_len, u_ref, x0_ref, ha_ref, hb_ref, hc_ref, bias_ref, fm_ref, ga_ref, gb_ref, o_ref):
    ha, hb, hc = ha_ref[...], hb_ref[...], hc_ref[...]
    bias = bias_ref[...]
    for s in range(u_ref.shape[0] // seq_len):
        rows = slice(s * seq_len, (s + 1) * seq_len)
        u = u_ref[rows, :]
        spec = jnp.dot(fm_ref[...], u, preferred_element_type=F32)
        a, b = spec[:seq_len], spec[seq_len:]
        ya = (a * ha - b * hb).astype(BF16)
        yb = (a * hb + b * hc).astype(BF16)
        y = (jnp.dot(ga_ref[...], ya, preferred_element_type=F32)
             + jnp.dot(gb_ref[...], yb, preferred_element_type=F32))
        v = y + u.astype(F32) * bias
        o_ref[rows, :] = (x0_ref[rows, :].astype(F32) * v).astype(BF16)


def _dft_conv(u, x0, ha, hb, hc, bias, fm, ga, gb, seq_len):
    t = u.shape[0]
    dt = 512
    row = pl.BlockSpec((ROW_TILE, dt), lambda j, i: (i, j))
    filt = pl.BlockSpec((seq_len, dt), lambda j, i: (0, j))
    return pl.pallas_call(
        functools.partial(_dft_conv_kernel, seq_len),
        grid=(D_MODEL // dt, t // ROW_TILE),
        in_specs=[row, row, filt, filt, filt, pl.BlockSpec((1, dt), lambda j, i: (0, j)),
                  _resident((2 * seq_len, seq_len)), _resident((seq_len, seq_len)),
                  _resident((seq_len, seq_len))],
        out_specs=row,
        out_shape=jax.ShapeDtypeStruct((t, D_MODEL), BF16),
        compiler_params=_params(2),
        name="hyena_dft_conv",
    )(u, x0, ha, hb, hc, bias, fm, ga, gb)


def _proj_ln_kernel(gate_idx, a_ref, x_ref, mod_ref, w_ref, b_ref, g_ref, beta_ref, o_ref):
    out = jnp.dot(a_ref[...], w_ref[...], preferred_element_type=F32) + b_ref[...]
    gate = mod_ref[0][gate_idx:gate_idx + 1]
    o_ref[...] = _residual_layer_norm(x_ref[...], out, gate, g_ref[...], beta_ref[...])


def _proj_ln(a, x, mods, w, b, g, beta, gate_idx, per_batch):
    t = x.shape[0]
    row = pl.BlockSpec((ROW_TILE, D_MODEL), lambda i: (i, 0))
    vec = _resident((1, D_MODEL))
    return pl.pallas_call(
        functools.partial(_proj_ln_kernel, gate_idx),
        grid=(t // ROW_TILE,),
        in_specs=[row, row, _mods_spec(per_batch), _resident((D_MODEL, D_MODEL)), vec, vec, vec],
        out_specs=row,
        out_shape=jax.ShapeDtypeStruct((t, D_MODEL), F32),
        compiler_params=_params(1),
        name="proj_ln",
    )(a, x, mods, w, b, g, beta)


def _gelu_tanh(x):
    return 0.5 * x * (1.0 + jnp.tanh(math.sqrt(2.0 / math.pi) * (x + 0.044715 * (x * x * x))))


def _ffn_kernel(seq_len, x_ref, mod_ref, wi_ref, bi_ref, cw_ref, cb_ref, wo_ref, bo_ref, g_ref, beta_ref,
                o_ref, acc_ref):
    m = mod_ref[0]
    x = x_ref[...]
    h = _modulate(x, m, 3).astype(BF16)
    first, last = _seq_bounds(h.shape[0], seq_len)
    acc_ref[...] = jnp.zeros_like(acc_ref)

    def chunk(c, carry):
        z = jnp.dot(h, wi_ref[c], preferred_element_type=F32) + bi_ref[c]
        zc = _dwconv3(z, cw_ref[c], cb_ref[c], first, last)
        a = (_gelu_tanh(zc[:, :FF_CHUNK]) * zc[:, FF_CHUNK:]).astype(BF16)
        acc_ref[...] += jnp.dot(a, wo_ref[c], preferred_element_type=F32)
        return carry

    lax.fori_loop(0, D_FF // FF_CHUNK, chunk, 0)
    out = acc_ref[...] + bo_ref[...]
    o_ref[...] = _residual_layer_norm(x, out, m[5:6], g_ref[...], beta_ref[...])


def _ffn(x, mods, wi, bi, cw, cb, wo, bo, g, beta, seq_len, per_batch):
    t = x.shape[0]
    n_c = D_FF // FF_CHUNK
    row = pl.BlockSpec((ROW_TILE, D_MODEL), lambda i: (i, 0))
    vec = _resident((1, D_MODEL))
    return pl.pallas_call(
        functools.partial(_ffn_kernel, seq_len),
        grid=(t // ROW_TILE,),
        in_specs=[row, _mods_spec(per_batch),
                  _resident((n_c, D_MODEL, 2 * FF_CHUNK)), _resident((n_c, 1, 2 * FF_CHUNK)),
                  _resident((n_c, 3, 2 * FF_CHUNK)), _resident((n_c, 1, 2 * FF_CHUNK)),
                  _resident((n_c, FF_CHUNK, D_MODEL)), vec, vec, vec],
        out_specs=row,
        out_shape=jax.ShapeDtypeStruct((t, D_MODEL), F32),
        scratch_shapes=[pltpu.VMEM((ROW_TILE, D_MODEL), F32)],
        compiler_params=_params(1),
        name="conv_ffn",
    )(x, mods, wi, bi, cw, cb, wo, bo, g, beta)


def _rms_norm(x, g):
    return x * lax.rsqrt(jnp.mean(x * x, axis=-1, keepdims=True) + QK_EPS) * g


def _qkv_kernel(rope, x_ref, mod_ref, w_ref, b_ref, qg_ref, kg_ref, *rest):
    if rope:
        cos_ref, sin_ref, q_ref, k_ref, v_ref = rest
        cos, sin = cos_ref[...], sin_ref[...]
    else:
        q_ref, k_ref, v_ref = rest
    h = _modulate(x_ref[...], mod_ref[0], 0).astype(BF16)
    z = jnp.dot(h, w_ref[...], preferred_element_type=F32) + b_ref[...]
    scale = HEAD_DIM ** -0.5

    def head(col, gain):
        y = _rms_norm(z[:, col:col + HEAD_DIM], gain)
        if rope:
            y = y * cos + pltpu.roll(y, HEAD_DIM // 2, 1) * sin
        return y

    for i in range(N_HEADS):
        q_ref[:, i * HEAD_DIM:(i + 1) * HEAD_DIM] = (head(i * HEAD_DIM, qg_ref[...]) * scale).astype(q_ref.dtype)
    for i in range(N_KV_HEADS):
        cols = slice(i * HEAD_DIM, (i + 1) * HEAD_DIM)
        k_ref[:, cols] = head(Q_DIM + i * HEAD_DIM, kg_ref[...]).astype(k_ref.dtype)
        v_ref[:, cols] = z[:, Q_DIM + KV_DIM + i * HEAD_DIM:Q_DIM + KV_DIM + (i + 1) * HEAD_DIM].astype(v_ref.dtype)


def _qkv(x, mods, w, b, qg, kg, rope_tables, kv_dtype, per_batch):
    t = x.shape[0]
    rope = rope_tables is not None
    row = pl.BlockSpec((ROW_TILE, D_MODEL), lambda i: (i, 0))
    kv_row = pl.BlockSpec((ROW_TILE, KV_DIM), lambda i: (i, 0))
    head_vec = _resident((1, HEAD_DIM))
    in_specs = [row, _mods_spec(per_batch), _resident((D_MODEL, QKV_DIM)), _resident((1, QKV_DIM)),
                head_vec, head_vec]
    args = [x, mods, w, b, qg, kg]
    if rope:
        in_specs += [_resident((ROW_TILE, HEAD_DIM))] * 2
        args += list(rope_tables)
    return pl.pallas_call(
        functools.partial(_qkv_kernel, rope),
        grid=(t // ROW_TILE,),
        in_specs=in_specs,
        out_specs=[row, kv_row, kv_row],
        out_shape=[jax.ShapeDtypeStruct((t, Q_DIM), BF16), jax.ShapeDtypeStruct((t, KV_DIM), kv_dtype),
                   jax.ShapeDtypeStruct((t, KV_DIM), kv_dtype)],
        compiler_params=_params(1),
        name="qkv_rope" if rope else "qkv",
    )(*args)


def _attn_kernel(has_ctx, q_ref, k_ref, v_ref, *rest):
    if has_ctx:
        kc_ref, vc_ref, o_ref = rest
        kc, vc = kc_ref[...], vc_ref[...]
    else:
        (o_ref,) = rest
    k = k_ref[...].astype(BF16)
    v = v_ref[...].astype(BF16)
    nt = (((1,), (1,)), ((), ()))
    for g in range(GROUP):
        cols = slice(g * HEAD_DIM, (g + 1) * HEAD_DIM)
        q = q_ref[:, cols]
        s = lax.dot_general(q, k, nt, preferred_element_type=F32)
        mx = jnp.max(s, axis=-1, keepdims=True)
        if has_ctx:
            sc = lax.dot_general(q, kc, nt, preferred_element_type=F32)
            mx = jnp.maximum(mx, jnp.max(sc, axis=-1, keepdims=True))
        p = jnp.exp(s - mx)
        den = jnp.sum(p, axis=-1, keepdims=True)
        o = jnp.dot(p.astype(BF16), v, preferred_element_type=F32)
        if has_ctx:
            pc = jnp.exp(sc - mx)
            den = den + jnp.sum(pc, axis=-1, keepdims=True)
            o = o + jnp.dot(pc.astype(BF16), vc, preferred_element_type=F32)
        o_ref[:, cols] = (o / den).astype(BF16)


def _attention(q, k, v, ctx, seq_len, q_tile):
    t = q.shape[0]
    n_b = t // seq_len
    n_q = seq_len // q_tile
    gw = GROUP * HEAD_DIM
    q_spec = pl.BlockSpec((q_tile, gw), lambda b, h, i: (b * n_q + i, h))
    kv_spec = pl.BlockSpec((seq_len, HEAD_DIM), lambda b, h, i: (b, h))
    in_specs = [q_spec, kv_spec, kv_spec]
    args = [q, k, v]
    if ctx is not None:
        kc, vc, layer = ctx
        ctx_spec = pl.BlockSpec((None, None, kc.shape[2], HEAD_DIM), lambda b, h, i: (b, layer, 0, h))
        in_specs += [ctx_spec, ctx_spec]
        args += [kc, vc]
    return pl.pallas_call(
        functools.partial(_attn_kernel, ctx is not None),
        grid=(n_b, N_KV_HEADS, n_q),
        in_specs=in_specs,
        out_specs=q_spec,
        out_shape=jax.ShapeDtypeStruct((t, Q_DIM), BF16),
        compiler_params=_params(3),
        name="attention_ctx" if ctx is not None else "attention",
    )(*args)


def _hyena_feats(seq_len):
    t = np.linspace(0.0, 1.0, seq_len)[:, None]
    n_bands = (POS_EMB_DIM - 1) // 2
    w = 2.0 * math.pi * np.arange(seq_len) / seq_len
    f = np.linspace(1e-4, n_bands - 1, n_bands)
    ang = w[:, None] * f[None, :]
    z = np.concatenate([t, np.cos(ang), -np.sin(ang)], -1)
    z = np.pad(z, ((0, 0), (0, FILTER_PAD - POS_EMB_DIM)))
    max_decay = math.log(DECAY_TARGET) / FAST_DECAY_PCT
    min_decay = math.log(DECAY_TARGET) / SLOW_DECAY_PCT
    deltas = np.abs(np.linspace(min_decay, max_decay, D_MODEL))[None, :]
    return z.astype(np.float32), t.astype(np.float32), deltas.astype(np.float32)


def _dft_mats(seq_len):
    n = 2 * seq_len
    idx = np.arange(seq_len)
    ang = 2.0 * math.pi * ((idx[:, None] * idx[None, :]) % n) / n
    sign = np.where(idx % 2 == 0, 1.0, -1.0)
    fa, fb = np.cos(ang), -np.sin(ang)
    fb[0, :] = sign
    fm = np.concatenate([fa, fb], 0)
    ga, gb = (2.0 / n) * np.cos(ang), -(2.0 / n) * np.sin(ang)
    ga[:, 0] = 1.0 / n
    gb[:, 0] = sign / n
    return fm.astype(np.float32), ga.astype(np.float32), gb.astype(np.float32)


def _rope_tables(seq_len):
    rows = np.repeat(np.arange(seq_len // GRID_W), GRID_W).astype(np.float64)
    cols = np.tile(np.arange(GRID_W), seq_len // GRID_W).astype(np.float64)
    half = HEAD_DIM // 2
    inv = ROPE_THETA ** (-np.arange(0, half, 2, dtype=np.float64) / half)
    ang = np.concatenate([rows[:, None] * inv, cols[:, None] * inv], -1)
    cos = np.concatenate([np.cos(ang), np.cos(ang)], -1)
    sin = np.concatenate([-np.sin(ang), np.sin(ang)], -1)
    return cos.astype(np.float32), sin.astype(np.float32)


def _deinterleave(a):
    return a.reshape(a.shape[:-1] + (HEAD_DIM // 2, 2)).swapaxes(-1, -2).reshape(a.shape)


def _chunk_cols(w, n_split, chunk):
    lead = w.shape[:-1]
    n_c = w.shape[-1] // (n_split * chunk)
    w = w.reshape(lead + (n_split, n_c, chunk))
    w = jnp.moveaxis(w, -2, 0)
    return w.reshape((n_c,) + lead + (n_split * chunk,))


def _trunk(x, mods_all, seq_len, per_batch, P, ctx):
    t = x.shape[0]
    fm, ga, gb = (jnp.asarray(a).astype(BF16) for a in _dft_mats(seq_len))
    feats, t_col, deltas = (jnp.asarray(a) for a in _hyena_feats(seq_len))
    new_k, new_v = [], []
    for l in range(DEPTH):
        mods = mods_all[l].reshape(COND_ROWS, N_MOD, D_MODEL)
        j = l // 2
        if l % 2 == 0:
            hp = P["hyena"][j]
            ha, hb, hc = _hyena_filter(feats, hp["w1"], hp["b1"], hp["w2"], hp["b2"], hp["freq"], hp["wout"],
                                       t_col, deltas, fm, seq_len)
            x0, u = _hyena_in(x, mods, hp["in_w"], hp["in_b"], hp["short_w"], hp["short_b"], seq_len, per_batch)
            a = _dft_conv(u, x0, ha, hb, hc, hp["filt_bias"], fm, ga, gb, seq_len)
            w_o, b_o = hp["out_w"], hp["out_b"]
        else:
            ap = P["attn"][j]
            if ctx is None:
                q, k, v = _qkv(x, mods, ap["w"], ap["b"], ap["qg"], ap["kg"], None, F32, per_batch)
                new_k.append(k)
                new_v.append(v)
                a = _attention(q, k, v, None, seq_len, seq_len)
            else:
                q, k, v = _qkv(x, mods, ap["w_rope"], ap["b_rope"], ap["qg_rope"], ap["kg_rope"],
                               P["rope"], BF16, per_batch)
                a = _attention(q, k, v, (ctx[0], ctx[1], j), seq_len, 256)
            w_o, b_o = ap["o_w"], ap["o_b"]
        x = _proj_ln(a, x, mods, w_o, b_o, P["ln_g"][l][0], P["ln_b"][l][0], 2, per_batch)
        fp = P["ffn"][l]
        x = _ffn(x, mods, fp["wi"], fp["bi"], fp["cw"], fp["cb"], fp["wo"], fp["bo"],
                 P["ln_g"][l][1], P["ln_b"][l][1], seq_len, per_batch)
    return x, new_k, new_v


def kernel(x_prompt, x_sample, cache_k, cache_v, c, c_ctx, w_mod, b_mod, ln_g, ln_b, hy_in_w, hy_in_b, hy_short_w, hy_short_b, hy_pos_w1, hy_pos_b1, hy_pos_w2, hy_pos_b2, hy_pos_wout, hy_freq, hy_filt_bias, hy_out_w, hy_out_b, at_qkv_w, at_qkv_b, at_q_gain, at_k_gain, at_o_w, at_o_b, ff_in_w, ff_in_b, ff_conv_w, ff_conv_b, ff_out_w, ff_out_b):
    batch, seq, _ = x_prompt.shape
    dec_batch, dec_seq, _ = x_sample.shape
    assert dec_seq == ROW_TILE and ROW_TILE % seq == 0 and 1 + dec_batch <= COND_ROWS
    n_attn = at_qkv_w.shape[0]
    pad = FILTER_PAD - FILTER_WIDTH

    P = {"hyena": [], "attn": [], "ffn": [], "rope": tuple(jnp.asarray(a) for a in _rope_tables(dec_seq))}
    P["ln_g"] = [[ln_g[l, i][None, :] for i in range(2)] for l in range(DEPTH)]
    P["ln_b"] = [[ln_b[l, i][None, :] for i in range(2)] for l in range(DEPTH)]
    for j in range(hy_in_w.shape[0]):
        P["hyena"].append({
            "in_w": _chunk_cols(hy_in_w[j].astype(BF16), 3, HY_CHUNK),
            "in_b": _chunk_cols(hy_in_b[j][None, :], 3, HY_CHUNK),
            "short_w": _chunk_cols(hy_short_w[j], 3, HY_CHUNK),
            "short_b": _chunk_cols(hy_short_b[j][None, :], 3, HY_CHUNK),
            "w1": jnp.pad(hy_pos_w1[j], ((0, FILTER_PAD - POS_EMB_DIM), (0, pad))),
            "b1": jnp.pad(hy_pos_b1[j], (0, pad))[None, :],
            "w2": jnp.pad(hy_pos_w2[j], ((0, 0), (0, pad), (0, pad))),
            "b2": jnp.pad(hy_pos_b2[j], ((0, 0), (0, pad)))[:, None, :],
            "freq": jnp.pad(hy_freq[j], (0, pad))[None, :],
            "wout": jnp.pad(hy_pos_wout[j], ((0, pad), (0, 0))),
            "filt_bias": hy_filt_bias[j][None, :],
            "out_w": hy_out_w[j].astype(BF16),
            "out_b": hy_out_b[j][None, :],
        })
    for j in range(n_attn):
        w, b = at_qkv_w[j], at_qkv_b[j]
        n_qk = (N_HEADS + N_KV_HEADS)
        w_qk = _deinterleave(w[:, :Q_DIM + KV_DIM].reshape(D_MODEL, n_qk, HEAD_DIM)).reshape(D_MODEL, -1)
        b_qk = _deinterleave(b[:Q_DIM + KV_DIM].reshape(n_qk, HEAD_DIM)).reshape(-1)
        P["attn"].append({
            "w": w.astype(BF16), "b": b[None, :],
            "qg": at_q_gain[j][None, :], "kg": at_k_gain[j][None, :],
            "w_rope": jnp.concatenate([w_qk, w[:, Q_DIM + KV_DIM:]], 1).astype(BF16),
            "b_rope": jnp.concatenate([b_qk, b[Q_DIM + KV_DIM:]])[None, :],
            "qg_rope": _deinterleave(at_q_gain[j])[None, :], "kg_rope": _deinterleave(at_k_gain[j])[None, :],
            "o_w": at_o_w[j].astype(BF16), "o_b": at_o_b[j][None, :],
        })
    for l in range(DEPTH):
        P["ffn"].append({
            "wi": _chunk_cols(ff_in_w[l].astype(BF16), 2, FF_CHUNK),
            "bi": _chunk_cols(ff_in_b[l][None, :], 2, FF_CHUNK),
            "cw": _chunk_cols(ff_conv_w[l], 2, FF_CHUNK),
            "cb": _chunk_cols(ff_conv_b[l][None, :], 2, FF_CHUNK),
            "wo": ff_out_w[l].astype(BF16).reshape(D_FF // FF_CHUNK, FF_CHUNK, D_MODEL),
            "bo": ff_out_b[l][None, :],
        })

    cond = jnp.concatenate([c_ctx[None, :], c, jnp.zeros((COND_ROWS - 1 - dec_batch, D_MODEL), F32)], 0)
    mods_all = _modulation_all(cond, w_mod, b_mod)

    past = cache_k.shape[2]
    ctx_k = _deinterleave(cache_k).astype(BF16).reshape(dec_batch, n_attn, past, KV_DIM)
    ctx_v = cache_v.astype(BF16).reshape(dec_batch, n_attn, past, KV_DIM)

    y_p, new_k, new_v = _trunk(x_prompt.reshape(batch * seq, D_MODEL), mods_all, seq, False, P, None)
    y_s, _, _ = _trunk(x_sample.reshape(dec_batch * dec_seq, D_MODEL), mods_all, dec_seq, True, P, (ctx_k, ctx_v))

    kv_shape = (batch, seq, N_KV_HEADS, HEAD_DIM)
    new_cache_k = jnp.stack([k.reshape(kv_shape) for k in new_k], axis=1)
    new_cache_v = jnp.stack([v.reshape(kv_shape) for v in new_v], axis=1)
    return (y_p.reshape(batch, seq, D_MODEL), y_s.reshape(dec_batch, dec_seq, D_MODEL), new_cache_k, new_cache_v)
```

```python
import functools
import math

import jax
import jax.numpy as jnp
import numpy as np
from jax import lax
from jax.experimental import pallas as pl
from jax.experimental.pallas import tpu as pltpu

D_MODEL = 1024
DEPTH = 4
GRID_W = 64
N_HEADS = 8
N_KV_HEADS = 2
HEAD_DIM = 128
GROUP = N_HEADS // N_KV_HEADS
Q_DIM = N_HEADS * HEAD_DIM
KV_DIM = N_KV_HEADS * HEAD_DIM
QKV_DIM = Q_DIM + 2 * KV_DIM
ROPE_THETA = 10000.0
QK_EPS = 1e-6
POS_EMB_DIM = 33
FILTER_WIDTH = 64
N_INNER_MLPS = 2
FAST_DECAY_PCT = 0.3
SLOW_DECAY_PCT = 1.5
DECAY_TARGET = 1e-2
MOD_SHIFT = 0.0
D_FF = 2816
LN_EPS = 1e-5
N_MOD = 6
DN_ALPHA = (2 * DEPTH) ** 0.25

F32 = jnp.float32
BF16 = jnp.bfloat16

LANES = 128
MXU_WIDTH = 256
ROW_TILE = 1024
FF_CHUNK = MXU_WIDTH
HY_CHUNK = MXU_WIDTH
PHASES = 4
PHASE_ROWS = ROW_TILE // PHASES
GELU_K = math.sqrt(2.0 / math.pi)
MOD_COL_TILE = 1536
COND_ROWS = 16
FILTER_PAD = LANES
VMEM_LIMIT = 56 * 1024 * 1024


def _params(n_axes):
    return pltpu.CompilerParams(dimension_semantics=("arbitrary",) * n_axes,
                                vmem_limit_bytes=VMEM_LIMIT)


def _resident(shape):
    nd = len(shape)
    return pl.BlockSpec(shape, lambda *_: (0,) * nd, pipeline_mode=pl.Buffered(1))


def _mods_spec(per_batch):
    if per_batch:
        return pl.BlockSpec((1, N_MOD, D_MODEL), lambda i: (i + 1, 0, 0))
    return pl.BlockSpec((1, N_MOD, D_MODEL), lambda i: (0, 0, 0))


def _modulate(x, m, shift_idx):
    return x * (1.0 + m[shift_idx + 1:shift_idx + 2]) + m[shift_idx:shift_idx + 1]


def _residual_layer_norm(x, out, gate, g, b):
    y = DN_ALPHA * x + (1.0 + gate) * out
    mu = jnp.mean(y, axis=-1, keepdims=True)
    yc = y - mu
    var = jnp.mean(yc * yc, axis=-1, keepdims=True)
    return yc * lax.rsqrt(var + LN_EPS) * g + b


def _phase_view(x):
    t = x.shape[0]
    return x.reshape(t // PHASES, PHASES * D_MODEL)


def _phase_bounds(seq_len):
    per_seq = seq_len // PHASES
    pos = lax.broadcasted_iota(jnp.int32, (PHASE_ROWS, 1), 0) & (per_seq - 1)
    return pos == 0, pos == per_seq - 1


def _phase_conv3(z, w, bias, conv_bias, seq_start, seq_end, scale=1.0):
    w = w * scale
    b_all = bias * (w[0:1] + w[1:2] + w[2:3]) + conv_bias * scale
    blk = [z[p * PHASE_ROWS:(p + 1) * PHASE_ROWS] for p in range(PHASES)]
    prev0 = jnp.where(seq_start, -bias, pltpu.roll(blk[PHASES - 1], 1, 0))
    next_last = jnp.where(seq_end, -bias, pltpu.roll(blk[0], PHASE_ROWS - 1, 0))
    out = []
    for p in range(PHASES):
        zp = prev0 if p == 0 else blk[p - 1]
        zn = next_last if p == PHASES - 1 else blk[p + 1]
        out.append(zp * w[0:1] + blk[p] * w[1:2] + zn * w[2:3] + b_all)
    return out


def _mod_kernel(cond_ref, w_ref, b_ref, o_ref):
    c = cond_ref[...]
    s = (c * jax.nn.sigmoid(c)).astype(BF16)
    o_ref[0] = jnp.dot(s, w_ref[0].astype(BF16), preferred_element_type=F32) + b_ref[0]


def _modulation_all(cond, w_mod, b_mod):
    n_col = (N_MOD * D_MODEL) // MOD_COL_TILE
    return pl.pallas_call(
        _mod_kernel,
        grid=(DEPTH, n_col),
        in_specs=[pl.BlockSpec((COND_ROWS, D_MODEL), lambda l, j: (0, 0)),
                  pl.BlockSpec((1, D_MODEL, MOD_COL_TILE), lambda l, j: (l, 0, j)),
                  pl.BlockSpec((1, 1, MOD_COL_TILE), lambda l, j: (l, 0, j))],
        out_specs=pl.BlockSpec((1, COND_ROWS, MOD_COL_TILE), lambda l, j: (l, 0, j)),
        out_shape=jax.ShapeDtypeStruct((DEPTH, COND_ROWS, N_MOD * D_MODEL), F32),
        compiler_params=_params(2),
        name="modulation",
    )(cond, w_mod, b_mod.reshape(DEPTH, 1, N_MOD * D_MODEL))


def _hyena_in_kernel(seq_len, x_ref, mod_ref, w_ref, b_ref, sw_ref, sb_ref, x0_ref, u_ref, h_ref):
    m = mod_ref[0]
    for p in range(PHASES):
        h_ref[p * PHASE_ROWS:(p + 1) * PHASE_ROWS, :] = _modulate(
            x_ref[:, p * D_MODEL:(p + 1) * D_MODEL], m, 0).astype(BF16)
    seq_start, seq_end = _phase_bounds(seq_len)
    for c in range(D_MODEL // HY_CHUNK):
        conv = []
        for part in range(3):
            cs = slice(part * D_MODEL + c * HY_CHUNK, part * D_MODEL + (c + 1) * HY_CHUNK)
            z = jnp.dot(h_ref[...], w_ref[:, cs], preferred_element_type=F32)
            conv.append(_phase_conv3(z, sw_ref[:, cs], b_ref[:, cs], sb_ref[:, cs], seq_start, seq_end))
        for p, (x0, x1, v) in enumerate(zip(*conv)):
            cs = slice(p * D_MODEL + c * HY_CHUNK, p * D_MODEL + (c + 1) * HY_CHUNK)
            x0_ref[:, cs] = x0.astype(BF16)
            u_ref[:, cs] = (v * x1).astype(BF16)


def _hyena_in(x, mods, w, b, sw, sb, seq_len, per_batch):
    t = x.shape[0]
    row = pl.BlockSpec((PHASE_ROWS, PHASES * D_MODEL), lambda i: (i, 0))
    x0, u = pl.pallas_call(
        functools.partial(_hyena_in_kernel, seq_len),
        grid=(t // ROW_TILE,),
        in_specs=[row, _mods_spec(per_batch),
                  _resident((D_MODEL, 3 * D_MODEL)), _resident((1, 3 * D_MODEL)),
                  _resident((3, 3 * D_MODEL)), _resident((1, 3 * D_MODEL))],
        out_specs=[row, row],
        out_shape=[jax.ShapeDtypeStruct((t // PHASES, PHASES * D_MODEL), BF16)] * 2,
        scratch_shapes=[pltpu.VMEM((ROW_TILE, D_MODEL), BF16)],
        compiler_params=_params(1),
        name="hyena_in",
    )(_phase_view(x), mods, w, b, sw, sb)
    return x0.reshape(t, D_MODEL), u.reshape(t, D_MODEL)


def _hyena_filter_kernel(seq_len, z_ref, w1_ref, b1_ref, w2_ref, b2_ref, freq_ref, wo0_ref, wo1_ref,
                         t_ref, delta_ref, fm_ref, ha_ref, hb_ref, hc_ref):
    hi = lax.Precision.HIGHEST
    freq = freq_ref[...]
    h = jnp.sin(freq * (jnp.dot(z_ref[...], w1_ref[...], precision=hi, preferred_element_type=F32)
                        + b1_ref[...]))
    for i in range(N_INNER_MLPS):
        h = jnp.sin(freq * (jnp.dot(h, w2_ref[i], precision=hi, preferred_element_type=F32) + b2_ref[i]))
    decay = jnp.exp(-t_ref[...] * delta_ref[...]) + MOD_SHIFT
    k0 = jnp.dot(h, wo0_ref[...], precision=hi, preferred_element_type=F32) * decay
    k1 = jnp.dot(h, wo1_ref[...], precision=hi, preferred_element_type=F32) * decay
    row0 = lax.broadcasted_iota(jnp.int32, (seq_len, 1), 0) == 0
    k1 = jnp.where(row0, 0.0, k1)
    fm = fm_ref[...]
    s0 = jnp.dot(fm, k0.astype(BF16), preferred_element_type=F32)
    s1 = jnp.dot(fm, k1.astype(BF16), preferred_element_type=F32)
    ha = s0[:seq_len] + s1[:seq_len]
    s0b, s1b = s0[seq_len:], s1[seq_len:]
    nyq = s0b + s1b
    ha_ref[...] = ha
    hb_ref[...] = jnp.where(row0, 0.0, s0b - s1b)
    hc_ref[...] = jnp.where(row0, nyq, ha)


def _hyena_filter(feats, w1, b1, w2, b2, freq, wout, t_col, deltas, fm, seq_len):
    dt = MXU_WIDTH
    n_d = D_MODEL // dt
    out = pl.BlockSpec((seq_len, dt), lambda j: (0, j))
    return pl.pallas_call(
        functools.partial(_hyena_filter_kernel, seq_len),
        grid=(n_d,),
        in_specs=[_resident((seq_len, FILTER_PAD)), _resident((FILTER_PAD, FILTER_PAD)),
                  _resident((1, FILTER_PAD)), _resident((N_INNER_MLPS, FILTER_PAD, FILTER_PAD)),
                  _resident((N_INNER_MLPS, 1, FILTER_PAD)), _resident((1, FILTER_PAD)),
                  pl.BlockSpec((FILTER_PAD, dt), lambda j: (0, j)),
                  pl.BlockSpec((FILTER_PAD, dt), lambda j: (0, n_d + j)),
                  _resident((seq_len, 1)), pl.BlockSpec((1, dt), lambda j: (0, j)),
                  _resident((2 * seq_len, seq_len))],
        out_specs=[out, out, out],
        out_shape=[jax.ShapeDtypeStruct((seq_len, D_MODEL), F32)] * 3,
        compiler_params=_params(1),
        name="hyena_filter",
    )(feats, w1, b1, w2, b2, freq, wout, wout, t_col, deltas, fm)


def _dft_conv_kernel(seq_len, u_ref, x0_ref, ha_ref, hb_ref, hc_ref, bias_ref, fm_ref, ga_ref, gb_ref, o_ref):
    ha, hb, hc = ha_ref[...], hb_ref[...], hc_ref[...]
    bias = bias_ref[...]
    for s in range(u_ref.shape[0] // seq_len):
        rows = slice(s * seq_len, (s + 1) * seq_len)
        u = u_ref[rows, :]
        spec = jnp.dot(fm_ref[...], u, preferred_element_type=F32)
        a, b = spec[:seq_len], spec[seq_len:]
        ya = (a * ha - b * hb).astype(BF16)
        yb = (a * hb + b * hc).astype(BF16)
        y = (jnp.dot(ga_ref[...], ya, preferred_element_type=F32)
             + jnp.dot(gb_ref[...], yb, preferred_element_type=F32))
        v = y + u.astype(F32) * bias
        o_ref[rows, :] = (x0_ref[rows, :].astype(F32) * v).astype(BF16)


def _dft_conv(u, x0, ha, hb, hc, bias, fm, ga, gb, seq_len):
    t = u.shape[0]
    dt = 512
    row = pl.BlockSpec((ROW_TILE, dt), lambda j, i: (i, j))
    filt = pl.BlockSpec((seq_len, dt), lambda j, i: (0, j))
    return pl.pallas_call(
        functools.partial(_dft_conv_kernel, seq_len),
        grid=(D_MODEL // dt, t // ROW_TILE),
        in_specs=[row, row, filt, filt, filt, pl.BlockSpec((1, dt), lambda j, i: (0, j)),
                  _resident((2 * seq_len, seq_len)), _resident((seq_len, seq_len)),
                  _resident((seq_len, seq_len))],
        out_specs=row,
        out_shape=jax.ShapeDtypeStruct((t, D_MODEL), BF16),
        compiler_params=_params(2),
        name="hyena_dft_conv",
    )(u, x0, ha, hb, hc, bias, fm, ga, gb)


def _proj_ln_kernel(gate_idx, a_ref, x_ref, mod_ref, w_ref, b_ref, g_ref, beta_ref, o_ref):
    out = jnp.dot(a_ref[...], w_ref[...], preferred_element_type=F32) + b_ref[...]
    gate = mod_ref[0][gate_idx:gate_idx + 1]
    o_ref[...] = _residual_layer_norm(x_ref[...], out, gate, g_ref[...], beta_ref[...])


def _proj_ln(a, x, mods, w, b, g, beta, gate_idx, per_batch):
    t = x.shape[0]
    row = pl.BlockSpec((ROW_TILE, D_MODEL), lambda i: (i, 0))
    vec = _resident((1, D_MODEL))
    return pl.pallas_call(
        functools.partial(_proj_ln_kernel, gate_idx),
        grid=(t // ROW_TILE,),
        in_specs=[row, row, _mods_spec(per_batch), _resident((D_MODEL, D_MODEL)), vec, vec, vec],
        out_specs=row,
        out_shape=jax.ShapeDtypeStruct((t, D_MODEL), F32),
        compiler_params=_params(1),
        name="proj_ln",
    )(a, x, mods, w, b, g, beta)


def _ffn_kernel(seq_len, x_ref, mod_ref, wi_ref, bi_ref, cw_ref, cb_ref, wo_ref, bo_ref, g_ref, beta_ref,
                o_ref, acc_ref, h_ref, z_ref):
    m = mod_ref[0]
    for p in range(PHASES):
        h_ref[p * PHASE_ROWS:(p + 1) * PHASE_ROWS, :] = _modulate(
            x_ref[:, p * D_MODEL:(p + 1) * D_MODEL], m, 3).astype(BF16)
    seq_start, seq_end = _phase_bounds(seq_len)
    n_c = D_FF // FF_CHUNK

    def cols(c, part):
        return pl.ds(pl.multiple_of(part * D_FF + c * FF_CHUNK, FF_CHUNK), FF_CHUNK)

    def first_matmul(c, slot):
        for part in range(2):
            z_ref[slot, part] = jnp.dot(h_ref[...], wi_ref[:, cols(c, part)], preferred_element_type=F32)

    def gate_and_second_matmul(c, slot, init=False):
        conv = [_phase_conv3(z_ref[slot, part], cw_ref[:, cols(c, part)], bi_ref[:, cols(c, part)],
                             cb_ref[:, cols(c, part)], seq_start, seq_end, scale=(1.0, 0.5)[part])
                for part in range(2)]
        a = []
        for g, half_v in zip(*conv):
            t = jnp.tanh(g * (GELU_K + (GELU_K * 0.044715) * (g * g)))
            gv = g * half_v
            a.append((gv + gv * t).astype(BF16))
        a = jnp.concatenate(a, axis=0)
        row = pl.multiple_of(c * FF_CHUNK, FF_CHUNK)
        upd = jnp.dot(a, wo_ref[pl.ds(row, FF_CHUNK), :], preferred_element_type=F32)
        if init:
            acc_ref[...] = upd
        else:
            acc_ref[...] += upd

    first_matmul(0, 0)
    first_matmul(1, 1)
    gate_and_second_matmul(0, 0, init=True)

    def pair(i, carry):
        c = 2 * i + 1
        first_matmul(c + 1, 0)
        gate_and_second_matmul(c, 1)
        first_matmul(c + 2, 1)
        gate_and_second_matmul(c + 1, 0)
        return carry

    lax.fori_loop(0, (n_c - 3) // 2, pair, 0)
    first_matmul(n_c - 1, 0)
    gate_and_second_matmul(n_c - 2, 1)
    gate_and_second_matmul(n_c - 1, 0)
    for p in range(PHASES):
        cs = slice(p * D_MODEL, (p + 1) * D_MODEL)
        out = acc_ref[p * PHASE_ROWS:(p + 1) * PHASE_ROWS, :] + bo_ref[...]
        o_ref[:, cs] = _residual_layer_norm(x_ref[:, cs], out, m[5:6], g_ref[...], beta_ref[...])


def _ffn(x, mods, wi, bi, cw, cb, wo, bo, g, beta, seq_len, per_batch):
    t = x.shape[0]
    assert (D_FF // FF_CHUNK) % 2 == 1
    row = pl.BlockSpec((PHASE_ROWS, PHASES * D_MODEL), lambda i: (i, 0))
    vec = _resident((1, D_MODEL))
    y = pl.pallas_call(
        functools.partial(_ffn_kernel, seq_len),
        grid=(t // ROW_TILE,),
        in_specs=[row, _mods_spec(per_batch),
                  _resident((D_MODEL, 2 * D_FF)), _resident((1, 2 * D_FF)),
                  _resident((3, 2 * D_FF)), _resident((1, 2 * D_FF)),
                  _resident((D_FF, D_MODEL)), vec, vec, vec],
        out_specs=row,
        out_shape=jax.ShapeDtypeStruct((t // PHASES, PHASES * D_MODEL), F32),
        scratch_shapes=[pltpu.VMEM((ROW_TILE, D_MODEL), F32), pltpu.VMEM((ROW_TILE, D_MODEL), BF16),
                        pltpu.VMEM((2, 2, ROW_TILE, FF_CHUNK), F32)],
        compiler_params=_params(1),
        name="conv_ffn",
    )(_phase_view(x), mods, wi, bi, cw, cb, wo, bo, g, beta)
    return y.reshape(t, D_MODEL)


def _rms_norm(x, g):
    return x * lax.rsqrt(jnp.mean(x * x, axis=-1, keepdims=True) + QK_EPS) * g


def _qkv_kernel(rope, x_ref, mod_ref, w_ref, b_ref, qg_ref, kg_ref, *rest):
    if rope:
        cos_ref, sin_ref, q_ref, k_ref, v_ref = rest
        cos, sin = cos_ref[...], sin_ref[...]
    else:
        q_ref, k_ref, v_ref = rest
    h = _modulate(x_ref[...], mod_ref[0], 0).astype(BF16)
    z = jnp.dot(h, w_ref[...], preferred_element_type=F32) + b_ref[...]
    scale = HEAD_DIM ** -0.5

    def head(col, gain):
        y = _rms_norm(z[:, col:col + HEAD_DIM], gain)
        if rope:
            y = y * cos + pltpu.roll(y, HEAD_DIM // 2, 1) * sin
        return y

    for i in range(N_HEADS):
        q_ref[:, i * HEAD_DIM:(i + 1) * HEAD_DIM] = (head(i * HEAD_DIM, qg_ref[...]) * scale).astype(q_ref.dtype)
    for i in range(N_KV_HEADS):
        cols = slice(i * HEAD_DIM, (i + 1) * HEAD_DIM)
        k_ref[:, cols] = head(Q_DIM + i * HEAD_DIM, kg_ref[...]).astype(k_ref.dtype)
        v_ref[:, cols] = z[:, Q_DIM + KV_DIM + i * HEAD_DIM:Q_DIM + KV_DIM + (i + 1) * HEAD_DIM].astype(v_ref.dtype)


def _qkv(x, mods, w, b, qg, kg, rope_tables, kv_dtype, per_batch):
    t = x.shape[0]
    rope = rope_tables is not None
    row = pl.BlockSpec((ROW_TILE, D_MODEL), lambda i: (i, 0))
    kv_row = pl.BlockSpec((ROW_TILE, KV_DIM), lambda i: (i, 0))
    head_vec = _resident((1, HEAD_DIM))
    in_specs = [row, _mods_spec(per_batch), _resident((D_MODEL, QKV_DIM)), _resident((1, QKV_DIM)),
                head_vec, head_vec]
    args = [x, mods, w, b, qg, kg]
    if rope:
        in_specs += [_resident((ROW_TILE, HEAD_DIM))] * 2
        args += list(rope_tables)
    return pl.pallas_call(
        functools.partial(_qkv_kernel, rope),
        grid=(t // ROW_TILE,),
        in_specs=in_specs,
        out_specs=[row, kv_row, kv_row],
        out_shape=[jax.ShapeDtypeStruct((t, Q_DIM), BF16), jax.ShapeDtypeStruct((t, KV_DIM), kv_dtype),
                   jax.ShapeDtypeStruct((t, KV_DIM), kv_dtype)],
        compiler_params=_params(1),
        name="qkv_rope" if rope else "qkv",
    )(*args)


def _attn_kernel(has_ctx, q_ref, k_ref, v_ref, *rest):
    if has_ctx:
        kc_ref, vc_ref, o_ref = rest
        kc, vc = kc_ref[...], vc_ref[...]
    else:
        (o_ref,) = rest
    k = k_ref[...].astype(BF16)
    v = v_ref[...].astype(BF16)
    nt = (((1,), (1,)), ((), ()))
    for g in range(GROUP):
        cols = slice(g * HEAD_DIM, (g + 1) * HEAD_DIM)
        q = q_ref[:, cols]
        s = lax.dot_general(q, k, nt, preferred_element_type=F32)
        mx = jnp.max(s, axis=-1, keepdims=True)
        if has_ctx:
            sc = lax.dot_general(q, kc, nt, preferred_element_type=F32)
            mx = jnp.maximum(mx, jnp.max(sc, axis=-1, keepdims=True))
        p = jnp.exp(s - mx)
        den = jnp.sum(p, axis=-1, keepdims=True)
        o = jnp.dot(p.astype(BF16), v, preferred_element_type=F32)
        if has_ctx:
            pc = jnp.exp(sc - mx)
            den = den + jnp.sum(pc, axis=-1, keepdims=True)
            o = o + jnp.dot(pc.astype(BF16), vc, preferred_element_type=F32)
        o_ref[:, cols] = (o / den).astype(BF16)


def _attention(q, k, v, ctx, seq_len, q_tile):
    t = q.shape[0]
    n_b = t // seq_len
    n_q = seq_len // q_tile
    gw = GROUP * HEAD_DIM
    q_spec = pl.BlockSpec((q_tile, gw), lambda b, h, i: (b * n_q + i, h))
    kv_spec = pl.BlockSpec((seq_len, HEAD_DIM), lambda b, h, i: (b, h))
    in_specs = [q_spec, kv_spec, kv_spec]
    args = [q, k, v]
    if ctx is not None:
        kc, vc, layer = ctx
        ctx_spec = pl.BlockSpec((None, None, kc.shape[2], HEAD_DIM), lambda b, h, i: (b, layer, 0, h))
        in_specs += [ctx_spec, ctx_spec]
        args += [kc, vc]
    return pl.pallas_call(
        functools.partial(_attn_kernel, ctx is not None),
        grid=(n_b, N_KV_HEADS, n_q),
        in_specs=in_specs,
        out_specs=q_spec,
        out_shape=jax.ShapeDtypeStruct((t, Q_DIM), BF16),
        compiler_params=_params(3),
        name="attention_ctx" if ctx is not None else "attention",
    )(*args)


def _hyena_feats(seq_len):
    t = np.linspace(0.0, 1.0, seq_len)[:, None]
    n_bands = (POS_EMB_DIM - 1) // 2
    w = 2.0 * math.pi * np.arange(seq_len) / seq_len
    f = np.linspace(1e-4, n_bands - 1, n_bands)
    ang = w[:, None] * f[None, :]
    z = np.concatenate([t, np.cos(ang), -np.sin(ang)], -1)
    z = np.pad(z, ((0, 0), (0, FILTER_PAD - POS_EMB_DIM)))
    max_decay = math.log(DECAY_TARGET) / FAST_DECAY_PCT
    min_decay = math.log(DECAY_TARGET) / SLOW_DECAY_PCT
    deltas = np.abs(np.linspace(min_decay, max_decay, D_MODEL))[None, :]
    return z.astype(np.float32), t.astype(np.float32), deltas.astype(np.float32)


def _dft_mats(seq_len):
    n = 2 * seq_len
    idx = np.arange(seq_len)
    ang = 2.0 * math.pi * ((idx[:, None] * idx[None, :]) % n) / n
    sign = np.where(idx % 2 == 0, 1.0, -1.0)
    fa, fb = np.cos(ang), -np.sin(ang)
    fb[0, :] = sign
    fm = np.concatenate([fa, fb], 0)
    ga, gb = (2.0 / n) * np.cos(ang), -(2.0 / n) * np.sin(ang)
    ga[:, 0] = 1.0 / n
    gb[:, 0] = sign / n
    return fm.astype(np.float32), ga.astype(np.float32), gb.astype(np.float32)


def _rope_tables(seq_len):
    rows = np.repeat(np.arange(seq_len // GRID_W), GRID_W).astype(np.float64)
    cols = np.tile(np.arange(GRID_W), seq_len // GRID_W).astype(np.float64)
    half = HEAD_DIM // 2
    inv = ROPE_THETA ** (-np.arange(0, half, 2, dtype=np.float64) / half)
    ang = np.concatenate([rows[:, None] * inv, cols[:, None] * inv], -1)
    cos = np.concatenate([np.cos(ang), np.cos(ang)], -1)
    sin = np.concatenate([-np.sin(ang), np.sin(ang)], -1)
    return cos.astype(np.float32), sin.astype(np.float32)


def _deinterleave(a):
    return a.reshape(a.shape[:-1] + (HEAD_DIM // 2, 2)).swapaxes(-1, -2).reshape(a.shape)


def _trunk(x, mods_all, seq_len, per_batch, P, ctx):
    t = x.shape[0]
    fm, ga, gb = (jnp.asarray(a).astype(BF16) for a in _dft_mats(seq_len))
    feats, t_col, deltas = (jnp.asarray(a) for a in _hyena_feats(seq_len))
    new_k, new_v = [], []
    for l in range(DEPTH):
        mods = mods_all[l].reshape(COND_ROWS, N_MOD, D_MODEL)
        j = l // 2
        if l % 2 == 0:
            hp = P["hyena"][j]
            ha, hb, hc = _hyena_filter(feats, hp["w1"], hp["b1"], hp["w2"], hp["b2"], hp["freq"], hp["wout"],
                                       t_col, deltas, fm, seq_len)
            x0, u = _hyena_in(x, mods, hp["in_w"], hp["in_b"], hp["short_w"], hp["short_b"], seq_len, per_batch)
            a = _dft_conv(u, x0, ha, hb, hc, hp["filt_bias"], fm, ga, gb, seq_len)
            w_o, b_o = hp["out_w"], hp["out_b"]
        else:
            ap = P["attn"][j]
            if ctx is None:
                q, k, v = _qkv(x, mods, ap["w"], ap["b"], ap["qg"], ap["kg"], None, F32, per_batch)
                new_k.append(k)
                new_v.append(v)
                a = _attention(q, k, v, None, seq_len, seq_len)
            else:
                q, k, v = _qkv(x, mods, ap["w_rope"], ap["b_rope"], ap["qg_rope"], ap["kg_rope"],
                               P["rope"], BF16, per_batch)
                a = _attention(q, k, v, (ctx[0], ctx[1], j), seq_len, 256)
            w_o, b_o = ap["o_w"], ap["o_b"]
        x = _proj_ln(a, x, mods, w_o, b_o, P["ln_g"][l][0], P["ln_b"][l][0], 2, per_batch)
        fp = P["ffn"][l]
        x = _ffn(x, mods, fp["wi"], fp["bi"], fp["cw"], fp["cb"], fp["wo"], fp["bo"],
                 P["ln_g"][l][1], P["ln_b"][l][1], seq_len, per_batch)
    return x, new_k, new_v


def kernel(x_prompt, x_sample, cache_k, cache_v, c, c_ctx, w_mod, b_mod, ln_g, ln_b, hy_in_w, hy_in_b, hy_short_w, hy_short_b, hy_pos_w1, hy_pos_b1, hy_pos_w2, hy_pos_b2, hy_pos_wout, hy_freq, hy_filt_bias, hy_out_w, hy_out_b, at_qkv_w, at_qkv_b, at_q_gain, at_k_gain, at_o_w, at_o_b, ff_in_w, ff_in_b, ff_conv_w, ff_conv_b, ff_out_w, ff_out_b):
    batch, seq, _ = x_prompt.shape
    dec_batch, dec_seq, _ = x_sample.shape
    assert dec_seq == ROW_TILE and ROW_TILE % seq == 0 and 1 + dec_batch <= COND_ROWS
    n_attn = at_qkv_w.shape[0]
    pad = FILTER_PAD - FILTER_WIDTH

    P = {"hyena": [], "attn": [], "ffn": [], "rope": tuple(jnp.asarray(a) for a in _rope_tables(dec_seq))}
    P["ln_g"] = [[ln_g[l, i][None, :] for i in range(2)] for l in range(DEPTH)]
    P["ln_b"] = [[ln_b[l, i][None, :] for i in range(2)] for l in range(DEPTH)]
    for j in range(hy_in_w.shape[0]):
        P["hyena"].append({
            "in_w": hy_in_w[j].astype(BF16),
            "in_b": hy_in_b[j][None, :],
            "short_w": hy_short_w[j],
            "short_b": hy_short_b[j][None, :],
            "w1": jnp.pad(hy_pos_w1[j], ((0, FILTER_PAD - POS_EMB_DIM), (0, pad))),
            "b1": jnp.pad(hy_pos_b1[j], (0, pad))[None, :],
            "w2": jnp.pad(hy_pos_w2[j], ((0, 0), (0, pad), (0, pad))),
            "b2": jnp.pad(hy_pos_b2[j], ((0, 0), (0, pad)))[:, None, :],
            "freq": jnp.pad(hy_freq[j], (0, pad))[None, :],
            "wout": jnp.pad(hy_pos_wout[j], ((0, pad), (0, 0))),
            "filt_bias": hy_filt_bias[j][None, :],
            "out_w": hy_out_w[j].astype(BF16),
            "out_b": hy_out_b[j][None, :],
        })
    for j in range(n_attn):
        w, b = at_qkv_w[j], at_qkv_b[j]
        n_qk = (N_HEADS + N_KV_HEADS)
        w_qk = _deinterleave(w[:, :Q_DIM + KV_DIM].reshape(D_MODEL, n_qk, HEAD_DIM)).reshape(D_MODEL, -1)
        b_qk = _deinterleave(b[:Q_DIM + KV_DIM].reshape(n_qk, HEAD_DIM)).reshape(-1)
        P["attn"].append({
            "w": w.astype(BF16), "b": b[None, :],
            "qg": at_q_gain[j][None, :], "kg": at_k_gain[j][None, :],
            "w_rope": jnp.concatenate([w_qk, w[:, Q_DIM + KV_DIM:]], 1).astype(BF16),
            "b_rope": jnp.concatenate([b_qk, b[Q_DIM + KV_DIM:]])[None, :],
            "qg_rope": _deinterleave(at_q_gain[j])[None, :], "kg_rope": _deinterleave(at_k_gain[j])[None, :],
            "o_w": at_o_w[j].astype(BF16), "o_b": at_o_b[j][None, :],
        })
    for l in range(DEPTH):
        P["ffn"].append({
            "wi": ff_in_w[l].astype(BF16),
            "bi": ff_in_b[l][None, :],
            "cw": ff_conv_w[l],
            "cb": ff_conv_b[l][None, :],
            "wo": ff_out_w[l].astype(BF16),
            "bo": ff_out_b[l][None, :],
        })

    cond = jnp.concatenate([c_ctx[None, :], c, jnp.zeros((COND_ROWS - 1 - dec_batch, D_MODEL), F32)], 0)
    mods_all = _modulation_all(cond, w_mod, b_mod)

    past = cache_k.shape[2]
    ctx_k = _deinterleave(cache_k).astype(BF16).reshape(dec_batch, n_attn, past, KV_DIM)
    ctx_v = cache_v.astype(BF16).reshape(dec_batch, n_attn, past, KV_DIM)

    y_p, new_k, new_v = _trunk(x_prompt.reshape(batch * seq, D_MODEL), mods_all, seq, False, P, None)
    y_s, _, _ = _trunk(x_sample.reshape(dec_batch * dec_seq, D_MODEL), mods_all, dec_seq, True, P, (ctx_k, ctx_v))

    kv_shape = (batch, seq, N_KV_HEADS, HEAD_DIM)
    new_cache_k = jnp.stack([k.reshape(kv_shape) for k in new_k], axis=1)
    new_cache_v = jnp.stack([v.reshape(kv_shape) for v in new_v], axis=1)
    return (y_p.reshape(batch, seq, D_MODEL), y_s.reshape(dec_batch, dec_seq, D_MODEL), new_cache_k, new_cache_v)
```

```python
import functools
import math

import jax
import jax.numpy as jnp
import numpy as np
from jax import lax
from jax.experimental import pallas as pl
from jax.experimental.pallas import tpu as pltpu

D_MODEL = 1024
DEPTH = 4
GRID_W = 64
N_HEADS = 8
N_KV_HEADS = 2
HEAD_DIM = 128
GROUP = N_HEADS // N_KV_HEADS
Q_DIM = N_HEADS * HEAD_DIM
KV_DIM = N_KV_HEADS * HEAD_DIM
QKV_DIM = Q_DIM + 2 * KV_DIM
ROPE_THETA = 10000.0
QK_EPS = 1e-6
POS_EMB_DIM = 33
FILTER_WIDTH = 64
N_INNER_MLPS = 2
FAST_DECAY_PCT = 0.3
SLOW_DECAY_PCT = 1.5
DECAY_TARGET = 1e-2
MOD_SHIFT = 0.0
D_FF = 2816
LN_EPS = 1e-5
N_MOD = 6
DN_ALPHA = (2 * DEPTH) ** 0.25

F32 = jnp.float32
BF16 = jnp.bfloat16

LANES = 128
MXU_WIDTH = 256
ROW_TILE = 1024
FF_CHUNK = MXU_WIDTH
HY_CHUNK = MXU_WIDTH
PHASES = 4
PHASE_ROWS = ROW_TILE // PHASES
GELU_K = math.sqrt(2.0 / math.pi)
MOD_COL_TILE = 1536
COND_ROWS = 16
FILTER_PAD = LANES
VMEM_LIMIT = 56 * 1024 * 1024


def _params(n_axes):
    return pltpu.CompilerParams(dimension_semantics=("arbitrary",) * n_axes,
                                vmem_limit_bytes=VMEM_LIMIT)


def _resident(shape):
    nd = len(shape)
    return pl.BlockSpec(shape, lambda *_: (0,) * nd, pipeline_mode=pl.Buffered(1))


def _mods_spec(per_batch):
    if per_batch:
        return pl.BlockSpec((1, N_MOD, D_MODEL), lambda i: (i + 1, 0, 0))
    return pl.BlockSpec((1, N_MOD, D_MODEL), lambda i: (0, 0, 0))


def _modulate(x, m, shift_idx):
    return x * (1.0 + m[shift_idx + 1:shift_idx + 2]) + m[shift_idx:shift_idx + 1]


def _residual_layer_norm(x, out, gate, g, b):
    y = DN_ALPHA * x + (1.0 + gate) * out
    mu = jnp.mean(y, axis=-1, keepdims=True)
    yc = y - mu
    var = jnp.mean(yc * yc, axis=-1, keepdims=True)
    return yc * lax.rsqrt(var + LN_EPS) * g + b


def _slab_fill(slab_ref, x_ref):
    for j in range(D_MODEL // LANES):
        slab_ref[j] = x_ref[:, j * LANES:(j + 1) * LANES]


def _slab_drain(slab_ref, o_ref):
    for j in range(D_MODEL // LANES):
        o_ref[:, j * LANES:(j + 1) * LANES] = slab_ref[j]


def _slab_read_phase(slab_ref, p):
    return jnp.concatenate([slab_ref[j, pl.ds(p, PHASE_ROWS, stride=PHASES), :]
                            for j in range(D_MODEL // LANES)], axis=1)


def _slab_write_phase(slab_ref, p, y):
    for j in range(D_MODEL // LANES):
        slab_ref[j, pl.ds(p, PHASE_ROWS, stride=PHASES), :] = y[:, j * LANES:(j + 1) * LANES]


def _slab_scratch():
    return pltpu.VMEM((D_MODEL // LANES, ROW_TILE, LANES), F32)


def _phase_bounds(seq_len):
    per_seq = seq_len // PHASES
    pos = lax.broadcasted_iota(jnp.int32, (PHASE_ROWS, 1), 0) & (per_seq - 1)
    return pos == 0, pos == per_seq - 1


def _phase_conv3(z, w, bias, conv_bias, seq_start, seq_end, scale=1.0):
    w = w * scale
    b_all = bias * (w[0:1] + w[1:2] + w[2:3]) + conv_bias * scale
    blk = [z[p * PHASE_ROWS:(p + 1) * PHASE_ROWS] for p in range(PHASES)]
    prev0 = jnp.where(seq_start, -bias, pltpu.roll(blk[PHASES - 1], 1, 0))
    next_last = jnp.where(seq_end, -bias, pltpu.roll(blk[0], PHASE_ROWS - 1, 0))
    out = []
    for p in range(PHASES):
        zp = prev0 if p == 0 else blk[p - 1]
        zn = next_last if p == PHASES - 1 else blk[p + 1]
        out.append(zp * w[0:1] + blk[p] * w[1:2] + zn * w[2:3] + b_all)
    return out


def _mod_kernel(cond_ref, w_ref, b_ref, o_ref):
    c = cond_ref[...]
    s = (c * jax.nn.sigmoid(c)).astype(BF16)
    o_ref[0] = jnp.dot(s, w_ref[0].astype(BF16), preferred_element_type=F32) + b_ref[0]


def _modulation_all(cond, w_mod, b_mod):
    n_col = (N_MOD * D_MODEL) // MOD_COL_TILE
    return pl.pallas_call(
        _mod_kernel,
        grid=(DEPTH, n_col),
        in_specs=[pl.BlockSpec((COND_ROWS, D_MODEL), lambda l, j: (0, 0)),
                  pl.BlockSpec((1, D_MODEL, MOD_COL_TILE), lambda l, j: (l, 0, j)),
                  pl.BlockSpec((1, 1, MOD_COL_TILE), lambda l, j: (l, 0, j))],
        out_specs=pl.BlockSpec((1, COND_ROWS, MOD_COL_TILE), lambda l, j: (l, 0, j)),
        out_shape=jax.ShapeDtypeStruct((DEPTH, COND_ROWS, N_MOD * D_MODEL), F32),
        compiler_params=_params(2),
        name="modulation",
    )(cond, w_mod, b_mod.reshape(DEPTH, 1, N_MOD * D_MODEL))


def _hyena_in_kernel(seq_len, x_ref, mod_ref, w_ref, b_ref, sw_ref, sb_ref, x0_ref, u_ref, h_ref, slab_ref):
    m = mod_ref[0]
    _slab_fill(slab_ref, x_ref)
    for p in range(PHASES):
        h_ref[p * PHASE_ROWS:(p + 1) * PHASE_ROWS, :] = _modulate(_slab_read_phase(slab_ref, p), m, 0).astype(BF16)
    seq_start, seq_end = _phase_bounds(seq_len)
    for c in range(D_MODEL // HY_CHUNK):
        conv = []
        for part in range(3):
            cs = slice(part * D_MODEL + c * HY_CHUNK, part * D_MODEL + (c + 1) * HY_CHUNK)
            z = jnp.dot(h_ref[...], w_ref[:, cs], preferred_element_type=F32)
            conv.append(_phase_conv3(z, sw_ref[:, cs], b_ref[:, cs], sb_ref[:, cs], seq_start, seq_end))
        cs = slice(c * HY_CHUNK, (c + 1) * HY_CHUNK)
        for p, (x0, x1, v) in enumerate(zip(*conv)):
            rs = slice(p * PHASE_ROWS, (p + 1) * PHASE_ROWS)
            x0_ref[rs, cs] = x0.astype(BF16)
            u_ref[rs, cs] = (v * x1).astype(BF16)


def _hyena_in(x, mods, w, b, sw, sb, seq_len, per_batch):
    t = x.shape[0]
    row = pl.BlockSpec((ROW_TILE, D_MODEL), lambda i: (i, 0))
    return pl.pallas_call(
        functools.partial(_hyena_in_kernel, seq_len),
        grid=(t // ROW_TILE,),
        in_specs=[row, _mods_spec(per_batch),
                  _resident((D_MODEL, 3 * D_MODEL)), _resident((1, 3 * D_MODEL)),
                  _resident((3, 3 * D_MODEL)), _resident((1, 3 * D_MODEL))],
        out_specs=[row, row],
        out_shape=[jax.ShapeDtypeStruct((t, D_MODEL), BF16)] * 2,
        scratch_shapes=[pltpu.VMEM((ROW_TILE, D_MODEL), BF16), _slab_scratch()],
        compiler_params=_params(1),
        name="hyena_in",
    )(x, mods, w, b, sw, sb)


def _hyena_filter_kernel(seq_len, z_ref, w1_ref, b1_ref, w2_ref, b2_ref, freq_ref, wo0_ref, wo1_ref,
                         t_ref, delta_ref, bias_ref, fm_ref, ha_ref, hb_ref, hc_ref, h_ref):
    hi = lax.Precision.HIGHEST

    @pl.when(pl.program_id(0) == 0)
    def _():
        freq = freq_ref[...]
        h = jnp.sin(freq * (jnp.dot(z_ref[...], w1_ref[...], precision=hi, preferred_element_type=F32)
                            + b1_ref[...]))
        for i in range(N_INNER_MLPS):
            h = jnp.sin(freq * (jnp.dot(h, w2_ref[i], precision=hi, preferred_element_type=F32) + b2_ref[i]))
        h_ref[...] = h

    h = h_ref[...]
    decay = jnp.exp(-t_ref[...] * delta_ref[...]) + MOD_SHIFT
    k0 = jnp.dot(h, wo0_ref[...], precision=hi, preferred_element_type=F32) * decay
    k1 = jnp.dot(h, wo1_ref[...], precision=hi, preferred_element_type=F32) * decay
    row0 = lax.broadcasted_iota(jnp.int32, (seq_len, 1), 0) == 0
    k0 = jnp.where(row0, k0 + bias_ref[...], k0)
    k1 = jnp.where(row0, 0.0, k1)
    fm = fm_ref[...]
    s0 = jnp.dot(fm, k0.astype(BF16), preferred_element_type=F32)
    s1 = jnp.dot(fm, k1.astype(BF16), preferred_element_type=F32)
    ha = s0[:seq_len] + s1[:seq_len]
    s0b, s1b = s0[seq_len:], s1[seq_len:]
    nyq = s0b + s1b
    ha_ref[...] = ha
    hb_ref[...] = jnp.where(row0, 0.0, s0b - s1b)
    hc_ref[...] = jnp.where(row0, nyq, ha)


def _hyena_filter(feats, w1, b1, w2, b2, freq, wout, t_col, deltas, bias, fm, seq_len):
    dt = MXU_WIDTH
    n_d = D_MODEL // dt
    out = pl.BlockSpec((seq_len, dt), lambda j: (0, j))
    vec = pl.BlockSpec((1, dt), lambda j: (0, j))
    return pl.pallas_call(
        functools.partial(_hyena_filter_kernel, seq_len),
        grid=(n_d,),
        in_specs=[_resident((seq_len, FILTER_PAD)), _resident((FILTER_PAD, FILTER_PAD)),
                  _resident((1, FILTER_PAD)), _resident((N_INNER_MLPS, FILTER_PAD, FILTER_PAD)),
                  _resident((N_INNER_MLPS, 1, FILTER_PAD)), _resident((1, FILTER_PAD)),
                  pl.BlockSpec((FILTER_PAD, dt), lambda j: (0, j)),
                  pl.BlockSpec((FILTER_PAD, dt), lambda j: (0, n_d + j)),
                  _resident((seq_len, 1)), vec, vec,
                  _resident((2 * seq_len, seq_len))],
        out_specs=[out, out, out],
        out_shape=[jax.ShapeDtypeStruct((seq_len, D_MODEL), F32)] * 3,
        scratch_shapes=[pltpu.VMEM((seq_len, FILTER_PAD), F32)],
        compiler_params=_params(1),
        name="hyena_filter",
    )(feats, w1, b1, w2, b2, freq, wout, wout, t_col, deltas, bias, fm)


def _dft_conv_kernel(seq_len, u_ref, x0_ref, ha_ref, hb_ref, hc_ref, fm_ref, ga_ref, gb_ref, o_ref):
    ha, hb, hc = ha_ref[...], hb_ref[...], hc_ref[...]
    per = seq_len // PHASES
    for s in range(ROW_TILE // seq_len):
        rows = [slice(p * PHASE_ROWS + s * per, p * PHASE_ROWS + (s + 1) * per) for p in range(PHASES)]
        u = jnp.concatenate([u_ref[r, :] for r in rows], axis=0)
        spec = jnp.dot(fm_ref[...], u, preferred_element_type=F32)
        a, b = spec[:seq_len], spec[seq_len:]
        ya = (a * ha - b * hb).astype(BF16)
        yb = (a * hb + b * hc).astype(BF16)
        y = (jnp.dot(ga_ref[...], ya, preferred_element_type=F32)
             + jnp.dot(gb_ref[...], yb, preferred_element_type=F32))
        for p, r in enumerate(rows):
            o_ref[r, :] = (x0_ref[r, :].astype(F32) * y[p * per:(p + 1) * per]).astype(BF16)


def _dft_conv(u, x0, ha, hb, hc, fm, ga, gb, seq_len):
    t = u.shape[0]
    dt = 512
    row = pl.BlockSpec((ROW_TILE, dt), lambda j, i: (i, j))
    filt = pl.BlockSpec((seq_len, dt), lambda j, i: (0, j))
    return pl.pallas_call(
        functools.partial(_dft_conv_kernel, seq_len),
        grid=(D_MODEL // dt, t // ROW_TILE),
        in_specs=[row, row, filt, filt, filt,
                  _resident((2 * seq_len, seq_len)), _resident((seq_len, seq_len)),
                  _resident((seq_len, seq_len))],
        out_specs=row,
        out_shape=jax.ShapeDtypeStruct((t, D_MODEL), BF16),
        compiler_params=_params(2),
        name="hyena_dft_conv",
    )(u, x0, ha, hb, hc, fm, ga, gb)


def _proj_ln_kernel(gate_idx, phase_major, a_ref, x_ref, mod_ref, w_ref, b_ref, g_ref, beta_ref, o_ref, *scratch):
    out = jnp.dot(a_ref[...], w_ref[...], preferred_element_type=F32) + b_ref[...]
    if phase_major:
        (slab_ref,) = scratch
        for p in range(PHASES):
            _slab_write_phase(slab_ref, p, out[p * PHASE_ROWS:(p + 1) * PHASE_ROWS])
        out = jnp.concatenate([slab_ref[j] for j in range(D_MODEL // LANES)], axis=1)
    gate = mod_ref[0][gate_idx:gate_idx + 1]
    o_ref[...] = _residual_layer_norm(x_ref[...], out, gate, g_ref[...], beta_ref[...])


def _proj_ln(a, x, mods, w, b, g, beta, gate_idx, per_batch, phase_major):
    t = x.shape[0]
    row = pl.BlockSpec((ROW_TILE, D_MODEL), lambda i: (i, 0))
    vec = _resident((1, D_MODEL))
    return pl.pallas_call(
        functools.partial(_proj_ln_kernel, gate_idx, phase_major),
        grid=(t // ROW_TILE,),
        in_specs=[row, row, _mods_spec(per_batch), _resident((D_MODEL, D_MODEL)), vec, vec, vec],
        out_specs=row,
        out_shape=jax.ShapeDtypeStruct((t, D_MODEL), F32),
        scratch_shapes=[_slab_scratch()] if phase_major else [],
        compiler_params=_params(1),
        name="proj_ln_phase" if phase_major else "proj_ln",
    )(a, x, mods, w, b, g, beta)


def _ffn_kernel(seq_len, x_ref, mod_ref, wi_ref, bi_ref, cw_ref, cb_ref, wo_ref, bo_ref, g_ref, beta_ref,
                o_ref, acc_ref, h_ref, z_ref, slab_ref):
    m = mod_ref[0]
    _slab_fill(slab_ref, x_ref)
    for p in range(PHASES):
        h_ref[p * PHASE_ROWS:(p + 1) * PHASE_ROWS, :] = _modulate(_slab_read_phase(slab_ref, p), m, 3).astype(BF16)
    seq_start, seq_end = _phase_bounds(seq_len)
    n_c = D_FF // FF_CHUNK

    def cols(c, part):
        return pl.ds(pl.multiple_of(part * D_FF + c * FF_CHUNK, FF_CHUNK), FF_CHUNK)

    def first_matmul(c, slot):
        for part in range(2):
            z_ref[slot, part] = jnp.dot(h_ref[...], wi_ref[:, cols(c, part)], preferred_element_type=F32)

    def gate_and_second_matmul(c, slot, init=False):
        conv = [_phase_conv3(z_ref[slot, part], cw_ref[:, cols(c, part)], bi_ref[:, cols(c, part)],
                             cb_ref[:, cols(c, part)], seq_start, seq_end, scale=(1.0, 0.5)[part])
                for part in range(2)]
        a = []
        for g, half_v in zip(*conv):
            t = jnp.tanh(g * (GELU_K + (GELU_K * 0.044715) * (g * g)))
            gv = g * half_v
            a.append((gv + gv * t).astype(BF16))
        a = jnp.concatenate(a, axis=0)
        row = pl.multiple_of(c * FF_CHUNK, FF_CHUNK)
        upd = jnp.dot(a, wo_ref[pl.ds(row, FF_CHUNK), :], preferred_element_type=F32)
        if init:
            acc_ref[...] = upd
        else:
            acc_ref[...] += upd

    first_matmul(0, 0)
    first_matmul(1, 1)
    gate_and_second_matmul(0, 0, init=True)

    def pair(i, carry):
        c = 2 * i + 1
        first_matmul(c + 1, 0)
        gate_and_second_matmul(c, 1)
        first_matmul(c + 2, 1)
        gate_and_second_matmul(c + 1, 0)
        return carry

    lax.fori_loop(0, (n_c - 3) // 2, pair, 0)
    first_matmul(n_c - 1, 0)
    gate_and_second_matmul(n_c - 2, 1)
    gate_and_second_matmul(n_c - 1, 0)
    for p in range(PHASES):
        out = acc_ref[p * PHASE_ROWS:(p + 1) * PHASE_ROWS, :] + bo_ref[...]
        y = _residual_layer_norm(_slab_read_phase(slab_ref, p), out, m[5:6], g_ref[...], beta_ref[...])
        _slab_write_phase(slab_ref, p, y)
    _slab_drain(slab_ref, o_ref)


def _ffn(x, mods, wi, bi, cw, cb, wo, bo, g, beta, seq_len, per_batch):
    t = x.shape[0]
    assert (D_FF // FF_CHUNK) % 2 == 1
    row = pl.BlockSpec((ROW_TILE, D_MODEL), lambda i: (i, 0))
    vec = _resident((1, D_MODEL))
    return pl.pallas_call(
        functools.partial(_ffn_kernel, seq_len),
        grid=(t // ROW_TILE,),
        in_specs=[row, _mods_spec(per_batch),
                  _resident((D_MODEL, 2 * D_FF)), _resident((1, 2 * D_FF)),
                  _resident((3, 2 * D_FF)), _resident((1, 2 * D_FF)),
                  _resident((D_FF, D_MODEL)), vec, vec, vec],
        out_specs=row,
        out_shape=jax.ShapeDtypeStruct((t, D_MODEL), F32),
        scratch_shapes=[pltpu.VMEM((ROW_TILE, D_MODEL), F32), pltpu.VMEM((ROW_TILE, D_MODEL), BF16),
                        pltpu.VMEM((2, 2, ROW_TILE, FF_CHUNK), F32), _slab_scratch()],
        compiler_params=_params(1),
        name="conv_ffn",
    )(x, mods, wi, bi, cw, cb, wo, bo, g, beta)


def _qkv_kernel(rope, x_ref, mod_ref, w_ref, b_ref, qg_ref, kg_ref, *rest):
    scale = HEAD_DIM ** -0.5
    h = _modulate(x_ref[...], mod_ref[0], 0).astype(BF16)
    z = jnp.dot(h, w_ref[:, :QKV_DIM], preferred_element_type=F32) + b_ref[:, :QKV_DIM]
    if rope:
        qgp_ref, kgp_ref, cos_ref, sin_ref, q_ref, k_ref, v_ref = rest
        zp = jnp.dot(h, w_ref[:, QKV_DIM:], preferred_element_type=F32) + b_ref[:, QKV_DIM:]
        cos, sin = cos_ref[...], sin_ref[...]
        tables = {"q": (qg_ref[...] * scale * cos, qgp_ref[...] * scale * sin),
                  "k": (kg_ref[...] * cos, kgp_ref[...] * sin)}
    else:
        q_ref, k_ref, v_ref = rest

    def head(col, kind):
        zi = z[:, col:col + HEAD_DIM]
        inv = lax.rsqrt(jnp.mean(zi * zi, axis=-1, keepdims=True) + QK_EPS)
        if rope:
            gc, gs = tables[kind]
            return inv * (zi * gc + zp[:, col:col + HEAD_DIM] * gs)
        if kind == "q":
            return zi * inv * (qg_ref[...] * scale)
        return zi * inv * kg_ref[...]

    for i in range(N_HEADS):
        q_ref[:, i * HEAD_DIM:(i + 1) * HEAD_DIM] = head(i * HEAD_DIM, "q").astype(q_ref.dtype)
    for i in range(N_KV_HEADS):
        cols = slice(i * HEAD_DIM, (i + 1) * HEAD_DIM)
        k_ref[:, cols] = head(Q_DIM + i * HEAD_DIM, "k").astype(k_ref.dtype)
        v_ref[:, cols] = z[:, Q_DIM + KV_DIM + i * HEAD_DIM:Q_DIM + KV_DIM + (i + 1) * HEAD_DIM].astype(v_ref.dtype)


def _qkv(x, mods, w, b, gains, rope_tables, kv_dtype, per_batch):
    t = x.shape[0]
    rope = rope_tables is not None
    row = pl.BlockSpec((ROW_TILE, D_MODEL), lambda i: (i, 0))
    kv_row = pl.BlockSpec((ROW_TILE, KV_DIM), lambda i: (i, 0))
    head_vec = _resident((1, HEAD_DIM))
    in_specs = [row, _mods_spec(per_batch), _resident(w.shape), _resident(b.shape)] + [head_vec] * len(gains)
    args = [x, mods, w, b] + list(gains)
    if rope:
        in_specs += [_resident((ROW_TILE, HEAD_DIM))] * 2
        args += list(rope_tables)
    return pl.pallas_call(
        functools.partial(_qkv_kernel, rope),
        grid=(t // ROW_TILE,),
        in_specs=in_specs,
        out_specs=[row, kv_row, kv_row],
        out_shape=[jax.ShapeDtypeStruct((t, Q_DIM), BF16), jax.ShapeDtypeStruct((t, KV_DIM), kv_dtype),
                   jax.ShapeDtypeStruct((t, KV_DIM), kv_dtype)],
        compiler_params=_params(1),
        name="qkv_rope" if rope else "qkv",
    )(*args)


def _attn_kernel(has_ctx, seq_len, q_ref, k_ref, v_ref, *rest):
    if has_ctx:
        kc_ref, vc_ref, o_ref = rest
        kc, vc = kc_ref[...], vc_ref[...]
    else:
        (o_ref,) = rest
    nt = (((1,), (1,)), ((), ()))
    n_seq = k_ref.shape[0] // seq_len
    q_rows = q_ref.shape[0] // n_seq
    for s in range(n_seq):
        k = k_ref[s * seq_len:(s + 1) * seq_len, :].astype(BF16)
        v = v_ref[s * seq_len:(s + 1) * seq_len, :].astype(BF16)
        rows = slice(s * q_rows, (s + 1) * q_rows)
        for g in range(GROUP):
            cols = slice(g * HEAD_DIM, (g + 1) * HEAD_DIM)
            q = q_ref[rows, cols]
            sc_new = lax.dot_general(q, k, nt, preferred_element_type=F32)
            mx = jnp.max(sc_new, axis=-1, keepdims=True)
            if has_ctx:
                sc_ctx = lax.dot_general(q, kc, nt, preferred_element_type=F32)
                mx = jnp.maximum(mx, jnp.max(sc_ctx, axis=-1, keepdims=True))
            p = jnp.exp(sc_new - mx)
            den = jnp.sum(p, axis=-1, keepdims=True)
            o = jnp.dot(p.astype(BF16), v, preferred_element_type=F32)
            if has_ctx:
                pc = jnp.exp(sc_ctx - mx)
                den = den + jnp.sum(pc, axis=-1, keepdims=True)
                o = o + jnp.dot(pc.astype(BF16), vc, preferred_element_type=F32)
            o_ref[rows, cols] = (o / den).astype(BF16)


def _attention(q, k, v, ctx, seq_len, q_tile):
    t = q.shape[0]
    gw = GROUP * HEAD_DIM
    kv_rows = max(seq_len, q_tile)
    n_b = t // kv_rows
    n_q = kv_rows // q_tile
    q_spec = pl.BlockSpec((q_tile, gw), lambda b, h, i: (b * n_q + i, h))
    kv_spec = pl.BlockSpec((kv_rows, HEAD_DIM), lambda b, h, i: (b, h))
    in_specs = [q_spec, kv_spec, kv_spec]
    args = [q, k, v]
    if ctx is not None:
        kc, vc, layer = ctx
        ctx_spec = pl.BlockSpec((None, None, kc.shape[2], HEAD_DIM), lambda b, h, i: (b, layer, 0, h))
        in_specs += [ctx_spec, ctx_spec]
        args += [kc, vc]
    return pl.pallas_call(
        functools.partial(_attn_kernel, ctx is not None, seq_len),
        grid=(n_b, N_KV_HEADS, n_q),
        in_specs=in_specs,
        out_specs=q_spec,
        out_shape=jax.ShapeDtypeStruct((t, Q_DIM), BF16),
        compiler_params=_params(3),
        name="attention_ctx" if ctx is not None else "attention",
    )(*args)


def _hyena_feats(seq_len):
    t = np.linspace(0.0, 1.0, seq_len)[:, None]
    n_bands = (POS_EMB_DIM - 1) // 2
    w = 2.0 * math.pi * np.arange(seq_len) / seq_len
    f = np.linspace(1e-4, n_bands - 1, n_bands)
    ang = w[:, None] * f[None, :]
    z = np.concatenate([t, np.cos(ang), -np.sin(ang)], -1)
    z = np.pad(z, ((0, 0), (0, FILTER_PAD - POS_EMB_DIM)))
    max_decay = math.log(DECAY_TARGET) / FAST_DECAY_PCT
    min_decay = math.log(DECAY_TARGET) / SLOW_DECAY_PCT
    deltas = np.abs(np.linspace(min_decay, max_decay, D_MODEL))[None, :]
    return z.astype(np.float32), t.astype(np.float32), deltas.astype(np.float32)


def _dft_mats(seq_len):
    n = 2 * seq_len
    idx = np.arange(seq_len)
    ang = 2.0 * math.pi * ((idx[:, None] * idx[None, :]) % n) / n
    sign = np.where(idx % 2 == 0, 1.0, -1.0)
    fa, fb = np.cos(ang), -np.sin(ang)
    fb[0, :] = sign
    fm = np.concatenate([fa, fb], 0)
    ga, gb = (2.0 / n) * np.cos(ang), -(2.0 / n) * np.sin(ang)
    ga[:, 0] = 1.0 / n
    gb[:, 0] = sign / n
    time_of = (np.arange(seq_len) % (seq_len // PHASES)) * PHASES + np.arange(seq_len) // (seq_len // PHASES)
    return (fm.astype(np.float32), fm[:, time_of].astype(np.float32),
            ga[time_of, :].astype(np.float32), gb[time_of, :].astype(np.float32))


def _rope_tables(seq_len):
    rows = np.repeat(np.arange(seq_len // GRID_W), GRID_W).astype(np.float64)
    cols = np.tile(np.arange(GRID_W), seq_len // GRID_W).astype(np.float64)
    half = HEAD_DIM // 2
    inv = ROPE_THETA ** (-np.arange(0, half, 2, dtype=np.float64) / half)
    ang = np.concatenate([rows[:, None] * inv, cols[:, None] * inv], -1)
    cos = np.repeat(np.cos(ang), 2, axis=-1)
    sin = np.stack([-np.sin(ang), np.sin(ang)], axis=-1).reshape(seq_len, HEAD_DIM)
    return cos.astype(np.float32), sin.astype(np.float32)


def _swap_pairs(a):
    return a.reshape(a.shape[:-1] + (a.shape[-1] // 2, 2))[..., ::-1].reshape(a.shape)


def _trunk(x, mods_all, seq_len, per_batch, P, ctx):
    fm, fm_pm, ga_pm, gb_pm = (jnp.asarray(a).astype(BF16) for a in _dft_mats(seq_len))
    feats, t_col, deltas = (jnp.asarray(a) for a in _hyena_feats(seq_len))
    new_k, new_v = [], []
    for l in range(DEPTH):
        mods = mods_all[l].reshape(COND_ROWS, N_MOD, D_MODEL)
        j = l // 2
        if l % 2 == 0:
            hp = P["hyena"][j]
            ha, hb, hc = _hyena_filter(feats, hp["w1"], hp["b1"], hp["w2"], hp["b2"], hp["freq"], hp["wout"],
                                       t_col, deltas, hp["filt_bias"], fm, seq_len)
            x0, u = _hyena_in(x, mods, hp["in_w"], hp["in_b"], hp["short_w"], hp["short_b"], seq_len, per_batch)
            a = _dft_conv(u, x0, ha, hb, hc, fm_pm, ga_pm, gb_pm, seq_len)
            w_o, b_o = hp["out_w"], hp["out_b"]
        else:
            ap = P["attn"][j]
            if ctx is None:
                q, k, v = _qkv(x, mods, ap["w"], ap["b"], (ap["qg"], ap["kg"]), None, F32, per_batch)
                new_k.append(k)
                new_v.append(v)
                a = _attention(q, k, v, None, seq_len, ROW_TILE)
            else:
                q, k, v = _qkv(x, mods, ap["w_rope"], ap["b_rope"], (ap["qg"], ap["kg"], ap["qg_swap"], ap["kg_swap"]),
                               P["rope"], BF16, per_batch)
                a = _attention(q, k, v, (ctx[0], ctx[1], j), seq_len, 256)
            w_o, b_o = ap["o_w"], ap["o_b"]
        x = _proj_ln(a, x, mods, w_o, b_o, P["ln_g"][l][0], P["ln_b"][l][0], 2, per_batch, l % 2 == 0)
        fp = P["ffn"][l]
        x = _ffn(x, mods, fp["wi"], fp["bi"], fp["cw"], fp["cb"], fp["wo"], fp["bo"],
                 P["ln_g"][l][1], P["ln_b"][l][1], seq_len, per_batch)
    return x, new_k, new_v


def kernel(x_prompt, x_sample, cache_k, cache_v, c, c_ctx, w_mod, b_mod, ln_g, ln_b, hy_in_w, hy_in_b, hy_short_w, hy_short_b, hy_pos_w1, hy_pos_b1, hy_pos_w2, hy_pos_b2, hy_pos_wout, hy_freq, hy_filt_bias, hy_out_w, hy_out_b, at_qkv_w, at_qkv_b, at_q_gain, at_k_gain, at_o_w, at_o_b, ff_in_w, ff_in_b, ff_conv_w, ff_conv_b, ff_out_w, ff_out_b):
    batch, seq, _ = x_prompt.shape
    dec_batch, dec_seq, _ = x_sample.shape
    assert dec_seq == ROW_TILE and ROW_TILE % seq == 0 and 1 + dec_batch <= COND_ROWS
    n_attn = at_qkv_w.shape[0]
    pad = FILTER_PAD - FILTER_WIDTH

    P = {"hyena": [], "attn": [], "ffn": [], "rope": tuple(jnp.asarray(a) for a in _rope_tables(dec_seq))}
    P["ln_g"] = [[ln_g[l, i][None, :] for i in range(2)] for l in range(DEPTH)]
    P["ln_b"] = [[ln_b[l, i][None, :] for i in range(2)] for l in range(DEPTH)]
    for j in range(hy_in_w.shape[0]):
        P["hyena"].append({
            "in_w": hy_in_w[j].astype(BF16),
            "in_b": hy_in_b[j][None, :],
            "short_w": hy_short_w[j],
            "short_b": hy_short_b[j][None, :],
            "w1": jnp.pad(hy_pos_w1[j], ((0, FILTER_PAD - POS_EMB_DIM), (0, pad))),
            "b1": jnp.pad(hy_pos_b1[j], (0, pad))[None, :],
            "w2": jnp.pad(hy_pos_w2[j], ((0, 0), (0, pad), (0, pad))),
            "b2": jnp.pad(hy_pos_b2[j], ((0, 0), (0, pad)))[:, None, :],
            "freq": jnp.pad(hy_freq[j], (0, pad))[None, :],
            "wout": jnp.pad(hy_pos_wout[j], ((0, pad), (0, 0))),
            "filt_bias": hy_filt_bias[j][None, :],
            "out_w": hy_out_w[j].astype(BF16),
            "out_b": hy_out_b[j][None, :],
        })
    for j in range(n_attn):
        w, b = at_qkv_w[j].astype(BF16), at_qkv_b[j]
        P["attn"].append({
            "w": w, "b": b[None, :],
            "qg": at_q_gain[j][None, :], "kg": at_k_gain[j][None, :],
            "w_rope": jnp.concatenate([w, _swap_pairs(w[:, :Q_DIM + KV_DIM])], 1),
            "b_rope": jnp.concatenate([b, _swap_pairs(b[:Q_DIM + KV_DIM])])[None, :],
            "qg_swap": _swap_pairs(at_q_gain[j])[None, :], "kg_swap": _swap_pairs(at_k_gain[j])[None, :],
            "o_w": at_o_w[j].astype(BF16), "o_b": at_o_b[j][None, :],
        })
    for l in range(DEPTH):
        P["ffn"].append({
            "wi": ff_in_w[l].astype(BF16),
            "bi": ff_in_b[l][None, :],
            "cw": ff_conv_w[l],
            "cb": ff_conv_b[l][None, :],
            "wo": ff_out_w[l].astype(BF16),
            "bo": ff_out_b[l][None, :],
        })

    cond = jnp.concatenate([c_ctx[None, :], c, jnp.zeros((COND_ROWS - 1 - dec_batch, D_MODEL), F32)], 0)
    mods_all = _modulation_all(cond, w_mod, b_mod)

    past = cache_k.shape[2]
    ctx_k = cache_k.astype(BF16).reshape(dec_batch, n_attn, past, KV_DIM)
    ctx_v = cache_v.astype(BF16).reshape(dec_batch, n_attn, past, KV_DIM)

    y_p, new_k, new_v = _trunk(x_prompt.reshape(batch * seq, D_MODEL), mods_all, seq, False, P, None)
    y_s, _, _ = _trunk(x_sample.reshape(dec_batch * dec_seq, D_MODEL), mods_all, dec_seq, True, P, (ctx_k, ctx_v))

    kv_shape = (batch, seq, N_KV_HEADS, HEAD_DIM)
    new_cache_k = jnp.stack([k.reshape(kv_shape) for k in new_k], axis=1)
    new_cache_v = jnp.stack([v.reshape(kv_shape) for v in new_v], axis=1)
    return (y_p.reshape(batch, seq, D_MODEL), y_s.reshape(dec_batch, dec_seq, D_MODEL), new_cache_k, new_cache_v)
```

```python
import functools
import math

import jax
import jax.numpy as jnp
import numpy as np
from jax import lax
from jax.experimental import pallas as pl
from jax.experimental.pallas import tpu as pltpu

D_MODEL = 1024
DEPTH = 4
GRID_W = 64
N_HEADS = 8
N_KV_HEADS = 2
HEAD_DIM = 128
GROUP = N_HEADS // N_KV_HEADS
Q_DIM = N_HEADS * HEAD_DIM
KV_DIM = N_KV_HEADS * HEAD_DIM
QK_DIM = Q_DIM + KV_DIM
QKV_DIM = Q_DIM + 2 * KV_DIM
ROPE_THETA = 10000.0
QK_EPS = 1e-6
POS_EMB_DIM = 33
FILTER_WIDTH = 64
N_INNER_MLPS = 2
FAST_DECAY_PCT = 0.3
SLOW_DECAY_PCT = 1.5
DECAY_TARGET = 1e-2
MOD_SHIFT = 0.0
D_FF = 2816
LN_EPS = 1e-5
N_MOD = 6
DN_ALPHA = (2 * DEPTH) ** 0.25

F32 = jnp.float32
BF16 = jnp.bfloat16

LANES = 128
SUBLANES = 8
MXU_WIDTH = 256
ROW_TILE = 1024
FF_CHUNK = MXU_WIDTH
HY_CHUNK = MXU_WIDTH
PHASES = 4
PHASE_ROWS = ROW_TILE // PHASES
GELU_K = math.sqrt(2.0 / math.pi)
LOG2_E = math.log2(math.e)
MOD_COL_TILE = 1536
COND_ROWS = 16
FILTER_PAD = LANES
VMEM_LIMIT = 56 * 1024 * 1024


def _params(n_axes):
    return pltpu.CompilerParams(dimension_semantics=("arbitrary",) * n_axes,
                                vmem_limit_bytes=VMEM_LIMIT)


def _resident(shape):
    nd = len(shape)
    return pl.BlockSpec(shape, lambda *_: (0,) * nd, pipeline_mode=pl.Buffered(1))


def _layer_block(arr, *lead):
    shape = arr.shape[len(lead):]
    idx = tuple(lead) + (0,) * len(shape)
    return pl.BlockSpec((None,) * len(lead) + shape, lambda *_: idx, pipeline_mode=pl.Buffered(1))


def _mods_spec(layer, per_batch):
    if per_batch:
        return pl.BlockSpec((None, SUBLANES, N_MOD * D_MODEL), lambda i: (layer, (i + 1) // SUBLANES, 0))
    return pl.BlockSpec((None, SUBLANES, N_MOD * D_MODEL), lambda i: (layer, 0, 0))


def _mod_row(mod_ref, per_batch):
    row = (pl.program_id(0) + 1) % SUBLANES if per_batch else 0
    return mod_ref[pl.ds(row, 1), :]


def _mod(m, idx):
    return m[:, idx * D_MODEL:(idx + 1) * D_MODEL]


def _modulate(x, m, shift_idx):
    return x * (1.0 + _mod(m, shift_idx + 1)) + _mod(m, shift_idx)


def _residual_layer_norm(x, out, gate, g, b):
    y = DN_ALPHA * x + (1.0 + gate) * out
    mu = jnp.mean(y, axis=-1, keepdims=True)
    yc = y - mu
    var = jnp.mean(yc * yc, axis=-1, keepdims=True)
    return yc * lax.rsqrt(var + LN_EPS) * g + b


def _slab_fill(slab_ref, x_ref):
    for j in range(D_MODEL // LANES):
        slab_ref[j] = x_ref[:, j * LANES:(j + 1) * LANES]


def _slab_drain(slab_ref, o_ref):
    for j in range(D_MODEL // LANES):
        o_ref[:, j * LANES:(j + 1) * LANES] = slab_ref[j]


def _slab_read_phase(slab_ref, p):
    return jnp.concatenate([slab_ref[j, pl.ds(p, PHASE_ROWS, stride=PHASES), :]
                            for j in range(D_MODEL // LANES)], axis=1)


def _slab_write_phase(slab_ref, p, y):
    for j in range(D_MODEL // LANES):
        slab_ref[j, pl.ds(p, PHASE_ROWS, stride=PHASES), :] = y[:, j * LANES:(j + 1) * LANES]


def _slab_scratch():
    return pltpu.VMEM((D_MODEL // LANES, ROW_TILE, LANES), F32)


def _phase_bounds(seq_len):
    per_seq = seq_len // PHASES
    pos = lax.broadcasted_iota(jnp.int32, (PHASE_ROWS, 1), 0) & (per_seq - 1)
    return pos == 0, pos == per_seq - 1


def _phase_conv3(z, w, bias, conv_bias, seq_start, seq_end, scale=1.0):
    w = w * scale
    b_all = bias * (w[0:1] + w[1:2] + w[2:3]) + conv_bias * scale
    blk = [z[p * PHASE_ROWS:(p + 1) * PHASE_ROWS] for p in range(PHASES)]
    prev0 = jnp.where(seq_start, -bias, pltpu.roll(blk[PHASES - 1], 1, 0))
    next_last = jnp.where(seq_end, -bias, pltpu.roll(blk[0], PHASE_ROWS - 1, 0))
    out = []
    for p in range(PHASES):
        zp = prev0 if p == 0 else blk[p - 1]
        zn = next_last if p == PHASES - 1 else blk[p + 1]
        out.append(zp * w[0:1] + blk[p] * w[1:2] + zn * w[2:3] + b_all)
    return out


def _mod_kernel(cond_ref, w_ref, b_ref, o_ref):
    c = cond_ref[...]
    s = (c * jax.nn.sigmoid(c)).astype(BF16)
    o_ref[0] = jnp.dot(s, w_ref[0].astype(BF16), preferred_element_type=F32) + b_ref[0]


def _modulation_all(cond, w_mod, b_mod):
    n_col = (N_MOD * D_MODEL) // MOD_COL_TILE
    return pl.pallas_call(
        _mod_kernel,
        grid=(DEPTH, n_col),
        in_specs=[pl.BlockSpec((COND_ROWS, D_MODEL), lambda l, j: (0, 0)),
                  pl.BlockSpec((1, D_MODEL, MOD_COL_TILE), lambda l, j: (l, 0, j)),
                  pl.BlockSpec((1, 1, MOD_COL_TILE), lambda l, j: (l, 0, j))],
        out_specs=pl.BlockSpec((1, COND_ROWS, MOD_COL_TILE), lambda l, j: (l, 0, j)),
        out_shape=jax.ShapeDtypeStruct((DEPTH, COND_ROWS, N_MOD * D_MODEL), F32),
        compiler_params=_params(2),
        name="modulation",
    )(cond, w_mod, b_mod.reshape(DEPTH, 1, N_MOD * D_MODEL))


def _hyena_in_kernel(seq_len, per_batch, x_ref, mod_ref, w_ref, b_ref, sw_ref, sb_ref, x0_ref, u_ref,
                     h_ref, slab_ref):
    m = _mod_row(mod_ref, per_batch)
    _slab_fill(slab_ref, x_ref)
    for p in range(PHASES):
        h_ref[p * PHASE_ROWS:(p + 1) * PHASE_ROWS, :] = _modulate(_slab_read_phase(slab_ref, p), m, 0).astype(BF16)
    seq_start, seq_end = _phase_bounds(seq_len)
    for c in range(D_MODEL // HY_CHUNK):
        conv = []
        for part in range(3):
            cs = slice(part * D_MODEL + c * HY_CHUNK, part * D_MODEL + (c + 1) * HY_CHUNK)
            z = jnp.dot(h_ref[...], w_ref[:, cs], preferred_element_type=F32)
            conv.append(_phase_conv3(z, sw_ref[:, cs], b_ref[:, cs], sb_ref[:, cs], seq_start, seq_end))
        cs = slice(c * HY_CHUNK, (c + 1) * HY_CHUNK)
        for p, (x0, x1, v) in enumerate(zip(*conv)):
            rs = slice(p * PHASE_ROWS, (p + 1) * PHASE_ROWS)
            x0_ref[rs, cs] = x0.astype(BF16)
            u_ref[rs, cs] = (v * x1).astype(BF16)


def _hyena_in(x, mods, W, j, layer, seq_len, per_batch):
    t = x.shape[0]
    row = pl.BlockSpec((ROW_TILE, D_MODEL), lambda i: (i, 0))
    return pl.pallas_call(
        functools.partial(_hyena_in_kernel, seq_len, per_batch),
        grid=(t // ROW_TILE,),
        in_specs=[row, _mods_spec(layer, per_batch),
                  _layer_block(W["hy_in_w"], j), _layer_block(W["hy_in_b"], j),
                  _layer_block(W["hy_short_w"], j), _layer_block(W["hy_short_b"], j)],
        out_specs=[row, row],
        out_shape=[jax.ShapeDtypeStruct((t, D_MODEL), BF16)] * 2,
        scratch_shapes=[pltpu.VMEM((ROW_TILE, D_MODEL), BF16), _slab_scratch()],
        compiler_params=_params(1),
        name="hyena_in",
    )(x, mods, W["hy_in_w"], W["hy_in_b"], W["hy_short_w"], W["hy_short_b"])


def _hyena_filter_kernel(seq_len, z_ref, w1_ref, b1_ref, w2_ref, b2_ref, freq_ref, wo0_ref, wo1_ref,
                         t_ref, delta_ref, bias_ref, fm_ref, ha_ref, hb_ref, hc_ref, h_ref):
    hi = lax.Precision.HIGHEST

    @pl.when(pl.program_id(0) == 0)
    def _():
        freq = freq_ref[...]
        h = jnp.sin(freq * (jnp.dot(z_ref[...], w1_ref[...], precision=hi, preferred_element_type=F32)
                            + b1_ref[...]))
        for i in range(N_INNER_MLPS):
            h = jnp.sin(freq * (jnp.dot(h, w2_ref[i], precision=hi, preferred_element_type=F32) + b2_ref[i]))
        h_ref[...] = h

    h = h_ref[...]
    decay = jnp.exp(-t_ref[...] * delta_ref[...]) + MOD_SHIFT
    zero_rows = jnp.zeros((FILTER_PAD - FILTER_WIDTH, wo0_ref.shape[1]), F32)

    def taps(wo_ref):
        wo = jnp.concatenate([wo_ref[...], zero_rows], axis=0)
        return jnp.dot(h, wo, precision=hi, preferred_element_type=F32) * decay

    k0, k1 = taps(wo0_ref), taps(wo1_ref)
    row0 = lax.broadcasted_iota(jnp.int32, (seq_len, 1), 0) == 0
    k0 = jnp.where(row0, k0 + bias_ref[...], k0)
    k1 = jnp.where(row0, 0.0, k1)
    fm = fm_ref[...]
    s0 = jnp.dot(fm, k0.astype(BF16), preferred_element_type=F32)
    s1 = jnp.dot(fm, k1.astype(BF16), preferred_element_type=F32)
    ha = s0[:seq_len] + s1[:seq_len]
    s0b, s1b = s0[seq_len:], s1[seq_len:]
    nyq = s0b + s1b
    ha_ref[...] = ha
    hb_ref[...] = jnp.where(row0, 0.0, s0b - s1b)
    hc_ref[...] = jnp.where(row0, nyq, ha)


def _hyena_filter(consts, W, j, seq_len):
    feats, t_col, deltas, fm = consts
    dt = MXU_WIDTH
    n_d = D_MODEL // dt
    out = pl.BlockSpec((seq_len, dt), lambda i: (0, i))
    wout = W["hy_pos_wout"]
    return pl.pallas_call(
        functools.partial(_hyena_filter_kernel, seq_len),
        grid=(n_d,),
        in_specs=[_resident((seq_len, FILTER_PAD)),
                  _layer_block(W["hy_pos_w1"], j), _layer_block(W["hy_pos_b1"], j),
                  _layer_block(W["hy_pos_w2"], j), _layer_block(W["hy_pos_b2"], j),
                  _layer_block(W["hy_freq"], j),
                  pl.BlockSpec((None, FILTER_WIDTH, dt), lambda i: (j, 0, i)),
                  pl.BlockSpec((None, FILTER_WIDTH, dt), lambda i: (j, 0, n_d + i)),
                  _resident((seq_len, 1)), pl.BlockSpec((1, dt), lambda i: (0, i)),
                  pl.BlockSpec((None, 1, dt), lambda i: (j, 0, i)),
                  _resident((2 * seq_len, seq_len))],
        out_specs=[out, out, out],
        out_shape=[jax.ShapeDtypeStruct((seq_len, D_MODEL), F32)] * 3,
        scratch_shapes=[pltpu.VMEM((seq_len, FILTER_PAD), F32)],
        compiler_params=_params(1),
        name="hyena_filter",
    )(feats, W["hy_pos_w1"], W["hy_pos_b1"], W["hy_pos_w2"], W["hy_pos_b2"], W["hy_freq"], wout, wout,
      t_col, deltas, W["hy_filt_bias"], fm)


def _dft_conv_kernel(seq_len, u_ref, x0_ref, ha_ref, hb_ref, hc_ref, fm_ref, ga_ref, gb_ref, o_ref):
    ha, hb, hc = ha_ref[...], hb_ref[...], hc_ref[...]
    per = seq_len // PHASES
    for s in range(ROW_TILE // seq_len):
        rows = [slice(p * PHASE_ROWS + s * per, p * PHASE_ROWS + (s + 1) * per) for p in range(PHASES)]
        u = jnp.concatenate([u_ref[r, :] for r in rows], axis=0)
        spec = jnp.dot(fm_ref[...], u, preferred_element_type=F32)
        a, b = spec[:seq_len], spec[seq_len:]
        ya = (a * ha - b * hb).astype(BF16)
        yb = (a * hb + b * hc).astype(BF16)
        y = (jnp.dot(ga_ref[...], ya, preferred_element_type=F32)
             + jnp.dot(gb_ref[...], yb, preferred_element_type=F32))
        for p, r in enumerate(rows):
            o_ref[r, :] = (x0_ref[r, :].astype(F32) * y[p * per:(p + 1) * per]).astype(BF16)


def _dft_conv(u, x0, ha, hb, hc, fm, ga, gb, seq_len):
    t = u.shape[0]
    dt = 512
    row = pl.BlockSpec((ROW_TILE, dt), lambda j, i: (i, j))
    filt = pl.BlockSpec((seq_len, dt), lambda j, i: (0, j))
    return pl.pallas_call(
        functools.partial(_dft_conv_kernel, seq_len),
        grid=(D_MODEL // dt, t // ROW_TILE),
        in_specs=[row, row, filt, filt, filt,
                  _resident((2 * seq_len, seq_len)), _resident((seq_len, seq_len)),
                  _resident((seq_len, seq_len))],
        out_specs=row,
        out_shape=jax.ShapeDtypeStruct((t, D_MODEL), BF16),
        compiler_params=_params(2),
        name="hyena_dft_conv",
    )(u, x0, ha, hb, hc, fm, ga, gb)


def _proj_ln_kernel(per_batch, phase_major, a_ref, x_ref, mod_ref, w_ref, b_ref, g_ref, beta_ref, o_ref, *scratch):
    out = jnp.dot(a_ref[...], w_ref[...], preferred_element_type=F32) + b_ref[...]
    if phase_major:
        (slab_ref,) = scratch
        for p in range(PHASES):
            _slab_write_phase(slab_ref, p, out[p * PHASE_ROWS:(p + 1) * PHASE_ROWS])
        out = jnp.concatenate([slab_ref[j] for j in range(D_MODEL // LANES)], axis=1)
    gate = _mod(_mod_row(mod_ref, per_batch), 2)
    o_ref[...] = _residual_layer_norm(x_ref[...], out, gate, g_ref[...], beta_ref[...])


def _proj_ln(a, x, mods, w, b, j, W, layer, per_batch, phase_major):
    t = x.shape[0]
    row = pl.BlockSpec((ROW_TILE, D_MODEL), lambda i: (i, 0))
    return pl.pallas_call(
        functools.partial(_proj_ln_kernel, per_batch, phase_major),
        grid=(t // ROW_TILE,),
        in_specs=[row, row, _mods_spec(layer, per_batch), _layer_block(w, j), _layer_block(b, j),
                  _layer_block(W["ln_g"], layer, 0), _layer_block(W["ln_b"], layer, 0)],
        out_specs=row,
        out_shape=jax.ShapeDtypeStruct((t, D_MODEL), F32),
        scratch_shapes=[_slab_scratch()] if phase_major else [],
        compiler_params=_params(1),
        name="proj_ln_phase" if phase_major else "proj_ln",
    )(a, x, mods, w, b, W["ln_g"], W["ln_b"])


def _ffn_kernel(seq_len, per_batch, x_ref, mod_ref, wi_ref, bi_ref, cw_ref, cb_ref, wo_ref, bo_ref, g_ref, beta_ref,
                o_ref, acc_ref, h_ref, z_ref, slab_ref):
    m = _mod_row(mod_ref, per_batch)
    _slab_fill(slab_ref, x_ref)
    for p in range(PHASES):
        h_ref[p * PHASE_ROWS:(p + 1) * PHASE_ROWS, :] = _modulate(_slab_read_phase(slab_ref, p), m, 3).astype(BF16)
    seq_start, seq_end = _phase_bounds(seq_len)
    n_c = D_FF // FF_CHUNK

    def cols(c, part):
        return pl.ds(pl.multiple_of(part * D_FF + c * FF_CHUNK, FF_CHUNK), FF_CHUNK)

    def first_matmul(c, slot):
        for part in range(2):
            z_ref[slot, part] = jnp.dot(h_ref[...], wi_ref[:, cols(c, part)], preferred_element_type=F32)

    def gate_and_second_matmul(c, slot, init=False):
        conv = [_phase_conv3(z_ref[slot, part], cw_ref[:, cols(c, part)], bi_ref[:, cols(c, part)],
                             cb_ref[:, cols(c, part)], seq_start, seq_end, scale=(1.0, 0.5)[part])
                for part in range(2)]
        a = []
        for g, half_v in zip(*conv):
            t = jnp.tanh(g * (GELU_K + (GELU_K * 0.044715) * (g * g)))
            gv = g * half_v
            a.append((gv + gv * t).astype(BF16))
        a = jnp.concatenate(a, axis=0)
        row = pl.multiple_of(c * FF_CHUNK, FF_CHUNK)
        upd = jnp.dot(a, wo_ref[pl.ds(row, FF_CHUNK), :], preferred_element_type=F32)
        if init:
            acc_ref[...] = upd
        else:
            acc_ref[...] += upd

    first_matmul(0, 0)
    first_matmul(1, 1)
    gate_and_second_matmul(0, 0, init=True)

    def pair(i, carry):
        c = 2 * i + 1
        first_matmul(c + 1, 0)
        gate_and_second_matmul(c, 1)
        first_matmul(c + 2, 1)
        gate_and_second_matmul(c + 1, 0)
        return carry

    lax.fori_loop(0, (n_c - 3) // 2, pair, 0)
    first_matmul(n_c - 1, 0)
    gate_and_second_matmul(n_c - 2, 1)
    gate_and_second_matmul(n_c - 1, 0)
    for p in range(PHASES):
        out = acc_ref[p * PHASE_ROWS:(p + 1) * PHASE_ROWS, :] + bo_ref[...]
        y = _residual_layer_norm(_slab_read_phase(slab_ref, p), out, _mod(m, 5), g_ref[...], beta_ref[...])
        _slab_write_phase(slab_ref, p, y)
    _slab_drain(slab_ref, o_ref)


def _ffn(x, mods, W, layer, seq_len, per_batch):
    t = x.shape[0]
    assert (D_FF // FF_CHUNK) % 2 == 1
    row = pl.BlockSpec((ROW_TILE, D_MODEL), lambda i: (i, 0))
    names = ["ff_in_w", "ff_in_b", "ff_conv_w", "ff_conv_b", "ff_out_w", "ff_out_b"]
    return pl.pallas_call(
        functools.partial(_ffn_kernel, seq_len, per_batch),
        grid=(t // ROW_TILE,),
        in_specs=[row, _mods_spec(layer, per_batch)] + [_layer_block(W[n], layer) for n in names]
                 + [_layer_block(W["ln_g"], layer, 1), _layer_block(W["ln_b"], layer, 1)],
        out_specs=row,
        out_shape=jax.ShapeDtypeStruct((t, D_MODEL), F32),
        scratch_shapes=[pltpu.VMEM((ROW_TILE, D_MODEL), F32), pltpu.VMEM((ROW_TILE, D_MODEL), BF16),
                        pltpu.VMEM((2, 2, ROW_TILE, FF_CHUNK), F32), _slab_scratch()],
        compiler_params=_params(1),
        name="conv_ffn",
    )(x, mods, *[W[n] for n in names], W["ln_g"], W["ln_b"])


def _qkv_kernel(rope, per_batch, x_ref, mod_ref, w_ref, b_ref, qg_ref, kg_ref, *rest):
    scale = LOG2_E * HEAD_DIM ** -0.5
    h = _modulate(x_ref[...], _mod_row(mod_ref, per_batch), 0).astype(BF16)
    z = jnp.dot(h, w_ref[...], preferred_element_type=F32) + b_ref[...]
    if rope:
        ws_ref, bs_ref, qgs_ref, kgs_ref, cos_ref, sin_ref, q_ref, k_ref, v_ref = rest
        zp = jnp.dot(h, ws_ref[...], preferred_element_type=F32) + bs_ref[...]
        cos, sin = cos_ref[...], sin_ref[...]
        tables = {"q": (qg_ref[...] * scale * cos, qgs_ref[...] * scale * sin),
                  "k": (kg_ref[...] * cos, kgs_ref[...] * sin)}
    else:
        q_ref, k_ref, v_ref = rest

    def head(col, kind):
        zi = z[:, col:col + HEAD_DIM]
        inv = lax.rsqrt(jnp.mean(zi * zi, axis=-1, keepdims=True) + QK_EPS)
        if rope:
            gc, gs = tables[kind]
            return inv * (zi * gc + zp[:, col:col + HEAD_DIM] * gs)
        if kind == "q":
            return zi * inv * (qg_ref[...] * scale)
        return zi * inv * kg_ref[...]

    for i in range(N_HEADS):
        q_ref[:, i * HEAD_DIM:(i + 1) * HEAD_DIM] = head(i * HEAD_DIM, "q").astype(q_ref.dtype)
    for i in range(N_KV_HEADS):
        cols = slice(i * HEAD_DIM, (i + 1) * HEAD_DIM)
        k_ref[:, cols] = head(Q_DIM + i * HEAD_DIM, "k").astype(k_ref.dtype)
        v_ref[:, cols] = z[:, QK_DIM + i * HEAD_DIM:QK_DIM + (i + 1) * HEAD_DIM].astype(v_ref.dtype)


def _qkv(x, mods, W, j, layer, rope_tables, kv_dtype, per_batch):
    t = x.shape[0]
    rope = rope_tables is not None
    row = pl.BlockSpec((ROW_TILE, D_MODEL), lambda i: (i, 0))
    kv_row = pl.BlockSpec((ROW_TILE, KV_DIM), lambda i: (i, 0))
    names = ["at_qkv_w", "at_qkv_b", "at_q_gain", "at_k_gain"]
    if rope:
        names += ["at_qk_w_swap", "at_qk_b_swap", "at_q_gain_swap", "at_k_gain_swap"]
    in_specs = [row, _mods_spec(layer, per_batch)] + [_layer_block(W[n], j) for n in names]
    args = [x, mods] + [W[n] for n in names]
    if rope:
        in_specs += [_resident((ROW_TILE, HEAD_DIM))] * 2
        args += list(rope_tables)
    return pl.pallas_call(
        functools.partial(_qkv_kernel, rope, per_batch),
        grid=(t // ROW_TILE,),
        in_specs=in_specs,
        out_specs=[row, kv_row, kv_row],
        out_shape=[jax.ShapeDtypeStruct((t, Q_DIM), BF16), jax.ShapeDtypeStruct((t, KV_DIM), kv_dtype),
                   jax.ShapeDtypeStruct((t, KV_DIM), kv_dtype)],
        compiler_params=_params(1),
        name="qkv_rope" if rope else "qkv",
    )(*args)


def _attn_kernel(has_ctx, seq_len, q_ref, k_ref, v_ref, *rest):
    if has_ctx:
        kc_ref, vc_ref, o_ref = rest
        kc = kc_ref[...]
        vc = jnp.concatenate([vc_ref[...], jnp.ones(vc_ref.shape, BF16)], axis=1)
    else:
        (o_ref,) = rest
    nt = (((1,), (1,)), ((), ()))
    n_seq = k_ref.shape[0] // seq_len
    q_rows = q_ref.shape[0] // n_seq
    ones = jnp.ones((seq_len, HEAD_DIM), BF16)
    for s in range(n_seq):
        k = k_ref[s * seq_len:(s + 1) * seq_len, :].astype(BF16)
        v = jnp.concatenate([v_ref[s * seq_len:(s + 1) * seq_len, :].astype(BF16), ones], axis=1)
        rows = slice(s * q_rows, (s + 1) * q_rows)
        for g in range(GROUP):
            cols = slice(g * HEAD_DIM, (g + 1) * HEAD_DIM)
            q = q_ref[rows, cols]
            sc_new = lax.dot_general(q, k, nt, preferred_element_type=F32)
            mx = jnp.max(sc_new, axis=-1, keepdims=True)
            if has_ctx:
                sc_ctx = lax.dot_general(q, kc, nt, preferred_element_type=F32)
                mx = jnp.maximum(mx, jnp.max(sc_ctx, axis=-1, keepdims=True))
            o = jnp.dot(jnp.exp2(sc_new - mx).astype(BF16), v, preferred_element_type=F32)
            if has_ctx:
                o = o + jnp.dot(jnp.exp2(sc_ctx - mx).astype(BF16), vc, preferred_element_type=F32)
            o_ref[rows, cols] = (o[:, :HEAD_DIM] / o[:, HEAD_DIM:]).astype(BF16)


def _attention(q, k, v, ctx, seq_len, q_tile):
    t = q.shape[0]
    gw = GROUP * HEAD_DIM
    kv_rows = max(seq_len, q_tile)
    n_b = t // kv_rows
    n_q = kv_rows // q_tile
    q_spec = pl.BlockSpec((q_tile, gw), lambda b, h, i: (b * n_q + i, h))
    kv_spec = pl.BlockSpec((kv_rows, HEAD_DIM), lambda b, h, i: (b, h))
    in_specs = [q_spec, kv_spec, kv_spec]
    args = [q, k, v]
    if ctx is not None:
        kc, vc, layer = ctx
        ctx_spec = pl.BlockSpec((None, None, kc.shape[2], HEAD_DIM), lambda b, h, i: (b, layer, 0, h))
        in_specs += [ctx_spec, ctx_spec]
        args += [kc, vc]
    return pl.pallas_call(
        functools.partial(_attn_kernel, ctx is not None, seq_len),
        grid=(n_b, N_KV_HEADS, n_q),
        in_specs=in_specs,
        out_specs=q_spec,
        out_shape=jax.ShapeDtypeStruct((t, Q_DIM), BF16),
        compiler_params=_params(3),
        name="attention_ctx" if ctx is not None else "attention",
    )(*args)


def _hyena_feats(seq_len):
    t = np.linspace(0.0, 1.0, seq_len)[:, None]
    n_bands = (POS_EMB_DIM - 1) // 2
    w = 2.0 * math.pi * np.arange(seq_len) / seq_len
    f = np.linspace(1e-4, n_bands - 1, n_bands)
    ang = w[:, None] * f[None, :]
    z = np.concatenate([t, np.cos(ang), -np.sin(ang)], -1)
    z = np.pad(z, ((0, 0), (0, FILTER_PAD - POS_EMB_DIM)))
    max_decay = math.log(DECAY_TARGET) / FAST_DECAY_PCT
    min_decay = math.log(DECAY_TARGET) / SLOW_DECAY_PCT
    deltas = np.abs(np.linspace(min_decay, max_decay, D_MODEL))[None, :]
    return z.astype(np.float32), t.astype(np.float32), deltas.astype(np.float32)


def _dft_mats(seq_len):
    n = 2 * seq_len
    idx = np.arange(seq_len)
    ang = 2.0 * math.pi * ((idx[:, None] * idx[None, :]) % n) / n
    sign = np.where(idx % 2 == 0, 1.0, -1.0)
    fa, fb = np.cos(ang), -np.sin(ang)
    fb[0, :] = sign
    fm = np.concatenate([fa, fb], 0)
    ga, gb = (2.0 / n) * np.cos(ang), -(2.0 / n) * np.sin(ang)
    ga[:, 0] = 1.0 / n
    gb[:, 0] = sign / n
    time_of = (np.arange(seq_len) % (seq_len // PHASES)) * PHASES + np.arange(seq_len) // (seq_len // PHASES)
    return (fm.astype(np.float32), fm[:, time_of].astype(np.float32),
            ga[time_of, :].astype(np.float32), gb[time_of, :].astype(np.float32))


def _rope_tables(seq_len):
    rows = np.repeat(np.arange(seq_len // GRID_W), GRID_W).astype(np.float64)
    cols = np.tile(np.arange(GRID_W), seq_len // GRID_W).astype(np.float64)
    half = HEAD_DIM // 2
    inv = ROPE_THETA ** (-np.arange(0, half, 2, dtype=np.float64) / half)
    ang = np.concatenate([rows[:, None] * inv, cols[:, None] * inv], -1)
    cos = np.repeat(np.cos(ang), 2, axis=-1)
    sin = np.stack([-np.sin(ang), np.sin(ang)], axis=-1).reshape(seq_len, HEAD_DIM)
    return cos.astype(np.float32), sin.astype(np.float32)


def _pair_swap_matrix(n):
    p = np.zeros((n, n), np.float32)
    idx = np.arange(n)
    p[idx ^ 1, idx] = 1.0
    return p


def _trunk(x, mods, seq_len, per_batch, W, rope, ctx):
    fm, fm_pm, ga_pm, gb_pm = (jnp.asarray(a).astype(BF16) for a in _dft_mats(seq_len))
    feats, t_col, deltas = (jnp.asarray(a) for a in _hyena_feats(seq_len))
    new_k, new_v = [], []
    for l in range(DEPTH):
        j = l // 2
        if l % 2 == 0:
            ha, hb, hc = _hyena_filter((feats, t_col, deltas, fm), W, j, seq_len)
            x0, u = _hyena_in(x, mods, W, j, l, seq_len, per_batch)
            a = _dft_conv(u, x0, ha, hb, hc, fm_pm, ga_pm, gb_pm, seq_len)
            w_o, b_o = W["hy_out_w"], W["hy_out_b"]
        else:
            if ctx is None:
                q, k, v = _qkv(x, mods, W, j, l, None, F32, per_batch)
                new_k.append(k)
                new_v.append(v)
                a = _attention(q, k, v, None, seq_len, ROW_TILE)
            else:
                q, k, v = _qkv(x, mods, W, j, l, rope, BF16, per_batch)
                a = _attention(q, k, v, (ctx[0], ctx[1], j), seq_len, 256)
            w_o, b_o = W["at_o_w"], W["at_o_b"]
        x = _proj_ln(a, x, mods, w_o, b_o, j, W, l, per_batch, l % 2 == 0)
        x = _ffn(x, mods, W, l, seq_len, per_batch)
    return x, new_k, new_v


def kernel(x_prompt, x_sample, cache_k, cache_v, c, c_ctx, w_mod, b_mod, ln_g, ln_b, hy_in_w, hy_in_b, hy_short_w, hy_short_b, hy_pos_w1, hy_pos_b1, hy_pos_w2, hy_pos_b2, hy_pos_wout, hy_freq, hy_filt_bias, hy_out_w, hy_out_b, at_qkv_w, at_qkv_b, at_q_gain, at_k_gain, at_o_w, at_o_b, ff_in_w, ff_in_b, ff_conv_w, ff_conv_b, ff_out_w, ff_out_b):
    batch, seq, _ = x_prompt.shape
    dec_batch, dec_seq, _ = x_sample.shape
    assert dec_seq == ROW_TILE and ROW_TILE % seq == 0 and 1 + dec_batch <= COND_ROWS
    n_attn = at_qkv_w.shape[0]
    pad = FILTER_PAD - FILTER_WIDTH

    def vec(a):
        return a[:, None, :]

    qkv_w = at_qkv_w.astype(BF16)
    swap = jnp.asarray(_pair_swap_matrix(QK_DIM))
    swap_head = jnp.asarray(_pair_swap_matrix(HEAD_DIM))
    hi = lax.Precision.HIGHEST
    W = {
        "ln_g": ln_g[:, :, None, :], "ln_b": ln_b[:, :, None, :],
        "hy_in_w": hy_in_w.astype(BF16), "hy_in_b": vec(hy_in_b),
        "hy_short_w": hy_short_w, "hy_short_b": vec(hy_short_b),
        "hy_pos_w1": jnp.pad(hy_pos_w1, ((0, 0), (0, FILTER_PAD - POS_EMB_DIM), (0, pad))),
        "hy_pos_b1": vec(jnp.pad(hy_pos_b1, ((0, 0), (0, pad)))),
        "hy_pos_w2": jnp.pad(hy_pos_w2, ((0, 0), (0, 0), (0, pad), (0, pad))),
        "hy_pos_b2": jnp.pad(hy_pos_b2, ((0, 0), (0, 0), (0, pad)))[:, :, None, :],
        "hy_freq": vec(jnp.pad(hy_freq, ((0, 0), (0, pad)))),
        "hy_pos_wout": hy_pos_wout, "hy_filt_bias": vec(hy_filt_bias),
        "hy_out_w": hy_out_w.astype(BF16), "hy_out_b": vec(hy_out_b),
        "at_qkv_w": qkv_w, "at_qkv_b": vec(at_qkv_b),
        "at_q_gain": vec(at_q_gain), "at_k_gain": vec(at_k_gain),
        "at_qk_w_swap": jnp.dot(qkv_w[:, :, :QK_DIM], swap.astype(BF16), preferred_element_type=F32).astype(BF16),
        "at_qk_b_swap": vec(jnp.dot(at_qkv_b[:, :QK_DIM], swap, precision=hi)),
        "at_q_gain_swap": vec(jnp.dot(at_q_gain, swap_head, precision=hi)),
        "at_k_gain_swap": vec(jnp.dot(at_k_gain, swap_head, precision=hi)),
        "at_o_w": at_o_w.astype(BF16), "at_o_b": vec(at_o_b),
        "ff_in_w": ff_in_w.astype(BF16), "ff_in_b": vec(ff_in_b),
        "ff_conv_w": ff_conv_w, "ff_conv_b": vec(ff_conv_b),
        "ff_out_w": ff_out_w.astype(BF16), "ff_out_b": vec(ff_out_b),
    }
    rope = tuple(jnp.asarray(a) for a in _rope_tables(dec_seq))

    cond = jnp.concatenate([c_ctx[None, :], c, jnp.zeros((COND_ROWS - 1 - dec_batch, D_MODEL), F32)], 0)
    mods = _modulation_all(cond, w_mod, b_mod)

    past = cache_k.shape[2]
    ctx_k = cache_k.astype(BF16).reshape(dec_batch, n_attn, past, KV_DIM)
    ctx_v = cache_v.astype(BF16).reshape(dec_batch, n_attn, past, KV_DIM)

    y_p, new_k, new_v = _trunk(x_prompt.reshape(batch * seq, D_MODEL), mods, seq, False, W, None, None)
    y_s, _, _ = _trunk(x_sample.reshape(dec_batch * dec_seq, D_MODEL), mods, dec_seq, True, W, rope, (ctx_k, ctx_v))

    kv_shape = (batch, seq, N_KV_HEADS, HEAD_DIM)
    new_cache_k = jnp.stack([k.reshape(kv_shape) for k in new_k], axis=1)
    new_cache_v = jnp.stack([v.reshape(kv_shape) for v in new_v], axis=1)
    return (y_p.reshape(batch, seq, D_MODEL), y_s.reshape(dec_batch, dec_seq, D_MODEL), new_cache_k, new_cache_v)
```

```python
import functools
import math

import jax
import jax.numpy as jnp
import numpy as np
from jax import lax
from jax.experimental import pallas as pl
from jax.experimental.pallas import tpu as pltpu

D_MODEL = 1024
DEPTH = 4
GRID_W = 64
N_HEADS = 8
N_KV_HEADS = 2
HEAD_DIM = 128
GROUP = N_HEADS // N_KV_HEADS
Q_DIM = N_HEADS * HEAD_DIM
KV_DIM = N_KV_HEADS * HEAD_DIM
QK_DIM = Q_DIM + KV_DIM
QKV_DIM = Q_DIM + 2 * KV_DIM
ROPE_THETA = 10000.0
QK_EPS = 1e-6
POS_EMB_DIM = 33
FILTER_WIDTH = 64
N_INNER_MLPS = 2
FAST_DECAY_PCT = 0.3
SLOW_DECAY_PCT = 1.5
DECAY_TARGET = 1e-2
MOD_SHIFT = 0.0
D_FF = 2816
LN_EPS = 1e-5
N_MOD = 6
DN_ALPHA = (2 * DEPTH) ** 0.25

F32 = jnp.float32
BF16 = jnp.bfloat16

LANES = 128
SUBLANES = 8
MXU_WIDTH = 256
ROW_TILE = 1024
FF_CHUNK = MXU_WIDTH
HY_CHUNK = MXU_WIDTH
PHASES = 4
PHASE_ROWS = ROW_TILE // PHASES
GELU_K = math.sqrt(2.0 / math.pi)
LOG2_E = math.log2(math.e)
HEAD_STACK = 1
CTX_Q_TILE = 512
MOD_COL_TILE = 1536
COND_ROWS = 16
FILTER_PAD = LANES
VMEM_LIMIT = 56 * 1024 * 1024


def _params(n_axes):
    return pltpu.CompilerParams(dimension_semantics=("arbitrary",) * n_axes,
                                vmem_limit_bytes=VMEM_LIMIT)


def _resident(shape):
    nd = len(shape)
    return pl.BlockSpec(shape, lambda *_: (0,) * nd, pipeline_mode=pl.Buffered(1))


def _layer_block(arr, *lead):
    shape = arr.shape[len(lead):]
    idx = tuple(lead) + (0,) * len(shape)
    return pl.BlockSpec((None,) * len(lead) + shape, lambda *_: idx, pipeline_mode=pl.Buffered(1))


def _mods_spec(layer, per_batch):
    if per_batch:
        return pl.BlockSpec((None, SUBLANES, N_MOD * D_MODEL), lambda i: (layer, (i + 1) // SUBLANES, 0))
    return pl.BlockSpec((None, SUBLANES, N_MOD * D_MODEL), lambda i: (layer, 0, 0))


def _mod_row(mod_ref, per_batch):
    row = (pl.program_id(0) + 1) % SUBLANES if per_batch else 0
    return mod_ref[pl.ds(row, 1), :]


def _mod(m, idx):
    return m[:, idx * D_MODEL:(idx + 1) * D_MODEL]


def _modulate(x, m, shift_idx):
    return x * (1.0 + _mod(m, shift_idx + 1)) + _mod(m, shift_idx)


def _residual_layer_norm(x, out, gate, g, b):
    y = DN_ALPHA * x + (1.0 + gate) * out
    mu = jnp.mean(y, axis=-1, keepdims=True)
    yc = y - mu
    var = jnp.mean(yc * yc, axis=-1, keepdims=True)
    return yc * lax.rsqrt(var + LN_EPS) * g + b


def _slab_fill(slab_ref, x_ref):
    for j in range(D_MODEL // LANES):
        slab_ref[j] = x_ref[:, j * LANES:(j + 1) * LANES]


def _slab_drain(slab_ref, o_ref):
    for j in range(D_MODEL // LANES):
        o_ref[:, j * LANES:(j + 1) * LANES] = slab_ref[j]


def _slab_read_phase(slab_ref, p):
    return jnp.concatenate([slab_ref[j, pl.ds(p, PHASE_ROWS, stride=PHASES), :]
                            for j in range(D_MODEL // LANES)], axis=1)


def _slab_write_phase(slab_ref, p, y):
    for j in range(D_MODEL // LANES):
        slab_ref[j, pl.ds(p, PHASE_ROWS, stride=PHASES), :] = y[:, j * LANES:(j + 1) * LANES]


def _slab_scratch():
    return pltpu.VMEM((D_MODEL // LANES, ROW_TILE, LANES), F32)


def _phase_bounds(seq_len):
    per_seq = seq_len // PHASES
    pos = lax.broadcasted_iota(jnp.int32, (PHASE_ROWS, 1), 0) & (per_seq - 1)
    return pos == 0, pos == per_seq - 1


def _phase_conv3(z, w, bias, conv_bias, seq_start, seq_end, scale=1.0):
    w = w * scale
    b_all = bias * (w[0:1] + w[1:2] + w[2:3]) + conv_bias * scale
    blk = [z[p * PHASE_ROWS:(p + 1) * PHASE_ROWS] for p in range(PHASES)]
    prev0 = jnp.where(seq_start, -bias, pltpu.roll(blk[PHASES - 1], 1, 0))
    next_last = jnp.where(seq_end, -bias, pltpu.roll(blk[0], PHASE_ROWS - 1, 0))
    out = []
    for p in range(PHASES):
        zp = prev0 if p == 0 else blk[p - 1]
        zn = next_last if p == PHASES - 1 else blk[p + 1]
        out.append(zp * w[0:1] + blk[p] * w[1:2] + zn * w[2:3] + b_all)
    return out


def _mod_kernel(cond_ref, w_ref, b_ref, o_ref):
    c = cond_ref[...]
    s = (c * jax.nn.sigmoid(c)).astype(BF16)
    o_ref[0] = jnp.dot(s, w_ref[0].astype(BF16), preferred_element_type=F32) + b_ref[0]


def _modulation_all(cond, w_mod, b_mod):
    n_col = (N_MOD * D_MODEL) // MOD_COL_TILE
    return pl.pallas_call(
        _mod_kernel,
        grid=(DEPTH, n_col),
        in_specs=[pl.BlockSpec((COND_ROWS, D_MODEL), lambda l, j: (0, 0)),
                  pl.BlockSpec((1, D_MODEL, MOD_COL_TILE), lambda l, j: (l, 0, j)),
                  pl.BlockSpec((1, 1, MOD_COL_TILE), lambda l, j: (l, 0, j))],
        out_specs=pl.BlockSpec((1, COND_ROWS, MOD_COL_TILE), lambda l, j: (l, 0, j)),
        out_shape=jax.ShapeDtypeStruct((DEPTH, COND_ROWS, N_MOD * D_MODEL), F32),
        compiler_params=_params(2),
        name="modulation",
    )(cond, w_mod, b_mod.reshape(DEPTH, 1, N_MOD * D_MODEL))


def _hyena_in_kernel(seq_len, per_batch, x_ref, mod_ref, w_ref, b_ref, sw_ref, sb_ref, x0_ref, u_ref,
                     h_ref, slab_ref):
    m = _mod_row(mod_ref, per_batch)
    _slab_fill(slab_ref, x_ref)
    for p in range(PHASES):
        h_ref[p * PHASE_ROWS:(p + 1) * PHASE_ROWS, :] = _modulate(_slab_read_phase(slab_ref, p), m, 0).astype(BF16)
    seq_start, seq_end = _phase_bounds(seq_len)
    for c in range(D_MODEL // HY_CHUNK):
        conv = []
        for part in range(3):
            cs = slice(part * D_MODEL + c * HY_CHUNK, part * D_MODEL + (c + 1) * HY_CHUNK)
            z = jnp.dot(h_ref[...], w_ref[:, cs], preferred_element_type=F32)
            conv.append(_phase_conv3(z, sw_ref[:, cs], b_ref[:, cs], sb_ref[:, cs], seq_start, seq_end))
        cs = slice(c * HY_CHUNK, (c + 1) * HY_CHUNK)
        for p, (x0, x1, v) in enumerate(zip(*conv)):
            rs = slice(p * PHASE_ROWS, (p + 1) * PHASE_ROWS)
            x0_ref[rs, cs] = x0.astype(BF16)
            u_ref[rs, cs] = (v * x1).astype(BF16)


def _hyena_in(x, mods, W, j, layer, seq_len, per_batch):
    t = x.shape[0]
    row = pl.BlockSpec((ROW_TILE, D_MODEL), lambda i: (i, 0))
    return pl.pallas_call(
        functools.partial(_hyena_in_kernel, seq_len, per_batch),
        grid=(t // ROW_TILE,),
        in_specs=[row, _mods_spec(layer, per_batch),
                  _layer_block(W["hy_in_w"], j), _layer_block(W["hy_in_b"], j),
                  _layer_block(W["hy_short_w"], j), _layer_block(W["hy_short_b"], j)],
        out_specs=[row, row],
        out_shape=[jax.ShapeDtypeStruct((t, D_MODEL), BF16)] * 2,
        scratch_shapes=[pltpu.VMEM((ROW_TILE, D_MODEL), BF16), _slab_scratch()],
        compiler_params=_params(1),
        name="hyena_in",
    )(x, mods, W["hy_in_w"], W["hy_in_b"], W["hy_short_w"], W["hy_short_b"])


def _hyena_filter_kernel(seq_len, z_ref, w1_ref, b1_ref, w2_ref, b2_ref, freq_ref, wo0_ref, wo1_ref,
                         t_ref, delta_ref, bias_ref, fm_ref, ha_ref, hb_ref, hc_ref, h_ref):
    hi = lax.Precision.HIGHEST

    @pl.when(pl.program_id(0) == 0)
    def _():
        freq = freq_ref[...]
        h = jnp.sin(freq * (jnp.dot(z_ref[...], w1_ref[...], precision=hi, preferred_element_type=F32)
                            + b1_ref[...]))
        for i in range(N_INNER_MLPS):
            h = jnp.sin(freq * (jnp.dot(h, w2_ref[i], precision=hi, preferred_element_type=F32) + b2_ref[i]))
        h_ref[...] = h

    h = h_ref[...]
    decay = jnp.exp(-t_ref[...] * delta_ref[...]) + MOD_SHIFT
    zero_rows = jnp.zeros((FILTER_PAD - FILTER_WIDTH, wo0_ref.shape[1]), F32)

    def taps(wo_ref):
        wo = jnp.concatenate([wo_ref[...], zero_rows], axis=0)
        return jnp.dot(h, wo, precision=hi, preferred_element_type=F32) * decay

    k0, k1 = taps(wo0_ref), taps(wo1_ref)
    row0 = lax.broadcasted_iota(jnp.int32, (seq_len, 1), 0) == 0
    k0 = jnp.where(row0, k0 + bias_ref[...], k0)
    k1 = jnp.where(row0, 0.0, k1)
    fm = fm_ref[...]
    s0 = jnp.dot(fm, k0.astype(BF16), preferred_element_type=F32)
    s1 = jnp.dot(fm, k1.astype(BF16), preferred_element_type=F32)
    ha = s0[:seq_len] + s1[:seq_len]
    s0b, s1b = s0[seq_len:], s1[seq_len:]
    nyq = s0b + s1b
    ha_ref[...] = ha
    hb_ref[...] = jnp.where(row0, 0.0, s0b - s1b)
    hc_ref[...] = jnp.where(row0, nyq, ha)


def _hyena_filter(consts, W, j, seq_len):
    feats, t_col, deltas, fm = consts
    dt = MXU_WIDTH
    n_d = D_MODEL // dt
    out = pl.BlockSpec((seq_len, dt), lambda i: (0, i))
    wout = W["hy_pos_wout"]
    return pl.pallas_call(
        functools.partial(_hyena_filter_kernel, seq_len),
        grid=(n_d,),
        in_specs=[_resident((seq_len, FILTER_PAD)),
                  _layer_block(W["hy_pos_w1"], j), _layer_block(W["hy_pos_b1"], j),
                  _layer_block(W["hy_pos_w2"], j), _layer_block(W["hy_pos_b2"], j),
                  _layer_block(W["hy_freq"], j),
                  pl.BlockSpec((None, FILTER_WIDTH, dt), lambda i: (j, 0, i)),
                  pl.BlockSpec((None, FILTER_WIDTH, dt), lambda i: (j, 0, n_d + i)),
                  _resident((seq_len, 1)), pl.BlockSpec((1, dt), lambda i: (0, i)),
                  pl.BlockSpec((None, 1, dt), lambda i: (j, 0, i)),
                  _resident((2 * seq_len, seq_len))],
        out_specs=[out, out, out],
        out_shape=[jax.ShapeDtypeStruct((seq_len, D_MODEL), F32)] * 3,
        scratch_shapes=[pltpu.VMEM((seq_len, FILTER_PAD), F32)],
        compiler_params=_params(1),
        name="hyena_filter",
    )(feats, W["hy_pos_w1"], W["hy_pos_b1"], W["hy_pos_w2"], W["hy_pos_b2"], W["hy_freq"], wout, wout,
      t_col, deltas, W["hy_filt_bias"], fm)


def _dft_conv_kernel(seq_len, u_ref, x0_ref, ha_ref, hb_ref, hc_ref, fm_ref, ga_ref, gb_ref, o_ref):
    ha, hb, hc = ha_ref[...], hb_ref[...], hc_ref[...]
    per = seq_len // PHASES
    for s in range(ROW_TILE // seq_len):
        rows = [slice(p * PHASE_ROWS + s * per, p * PHASE_ROWS + (s + 1) * per) for p in range(PHASES)]
        u = jnp.concatenate([u_ref[r, :] for r in rows], axis=0)
        spec = jnp.dot(fm_ref[...], u, preferred_element_type=F32)
        a, b = spec[:seq_len], spec[seq_len:]
        ya = (a * ha - b * hb).astype(BF16)
        yb = (a * hb + b * hc).astype(BF16)
        y = (jnp.dot(ga_ref[...], ya, preferred_element_type=F32)
             + jnp.dot(gb_ref[...], yb, preferred_element_type=F32))
        for p, r in enumerate(rows):
            o_ref[r, :] = (x0_ref[r, :].astype(F32) * y[p * per:(p + 1) * per]).astype(BF16)


def _dft_conv(u, x0, ha, hb, hc, fm, ga, gb, seq_len):
    t = u.shape[0]
    dt = 512 if seq_len == ROW_TILE else D_MODEL
    row = pl.BlockSpec((ROW_TILE, dt), lambda j, i: (i, j))
    filt = pl.BlockSpec((seq_len, dt), lambda j, i: (0, j))
    return pl.pallas_call(
        functools.partial(_dft_conv_kernel, seq_len),
        grid=(D_MODEL // dt, t // ROW_TILE),
        in_specs=[row, row, filt, filt, filt,
                  _resident((2 * seq_len, seq_len)), _resident((seq_len, seq_len)),
                  _resident((seq_len, seq_len))],
        out_specs=row,
        out_shape=jax.ShapeDtypeStruct((t, D_MODEL), BF16),
        compiler_params=_params(2),
        name="hyena_dft_conv",
    )(u, x0, ha, hb, hc, fm, ga, gb)


def _proj_ln_kernel(per_batch, phase_major, a_ref, x_ref, mod_ref, w_ref, b_ref, g_ref, beta_ref, o_ref, *scratch):
    out = jnp.dot(a_ref[...], w_ref[...], preferred_element_type=F32) + b_ref[...]
    if phase_major:
        (slab_ref,) = scratch
        for p in range(PHASES):
            _slab_write_phase(slab_ref, p, out[p * PHASE_ROWS:(p + 1) * PHASE_ROWS])
        out = jnp.concatenate([slab_ref[j] for j in range(D_MODEL // LANES)], axis=1)
    gate = _mod(_mod_row(mod_ref, per_batch), 2)
    o_ref[...] = _residual_layer_norm(x_ref[...], out, gate, g_ref[...], beta_ref[...])


def _proj_ln(a, x, mods, w, b, j, W, layer, per_batch, phase_major):
    t = x.shape[0]
    row = pl.BlockSpec((ROW_TILE, D_MODEL), lambda i: (i, 0))
    return pl.pallas_call(
        functools.partial(_proj_ln_kernel, per_batch, phase_major),
        grid=(t // ROW_TILE,),
        in_specs=[row, row, _mods_spec(layer, per_batch), _layer_block(w, j), _layer_block(b, j),
                  _layer_block(W["ln_g"], layer, 0), _layer_block(W["ln_b"], layer, 0)],
        out_specs=row,
        out_shape=jax.ShapeDtypeStruct((t, D_MODEL), F32),
        scratch_shapes=[_slab_scratch()] if phase_major else [],
        compiler_params=_params(1),
        name="proj_ln_phase" if phase_major else "proj_ln",
    )(a, x, mods, w, b, W["ln_g"], W["ln_b"])


def _ffn_kernel(seq_len, per_batch, x_ref, mod_ref, wi_ref, bi_ref, cw_ref, cb_ref, wo_ref, bo_ref, g_ref, beta_ref,
                o_ref, acc_ref, h_ref, z_ref, slab_ref):
    m = _mod_row(mod_ref, per_batch)
    _slab_fill(slab_ref, x_ref)
    for p in range(PHASES):
        h_ref[p * PHASE_ROWS:(p + 1) * PHASE_ROWS, :] = _modulate(_slab_read_phase(slab_ref, p), m, 3).astype(BF16)
    seq_start, seq_end = _phase_bounds(seq_len)
    n_c = D_FF // FF_CHUNK

    def cols(c, part):
        return pl.ds(pl.multiple_of(part * D_FF + c * FF_CHUNK, FF_CHUNK), FF_CHUNK)

    def first_matmul(c, slot):
        for part in range(2):
            z_ref[slot, part] = jnp.dot(h_ref[...], wi_ref[:, cols(c, part)], preferred_element_type=F32)

    def gate_and_second_matmul(c, slot, init=False):
        conv = [_phase_conv3(z_ref[slot, part], cw_ref[:, cols(c, part)], bi_ref[:, cols(c, part)],
                             cb_ref[:, cols(c, part)], seq_start, seq_end, scale=(1.0, 0.5)[part])
                for part in range(2)]
        a = []
        for g, half_v in zip(*conv):
            t = jnp.tanh(g * (GELU_K + (GELU_K * 0.044715) * (g * g)))
            gv = g * half_v
            a.append((gv + gv * t).astype(BF16))
        a = jnp.concatenate(a, axis=0)
        row = pl.multiple_of(c * FF_CHUNK, FF_CHUNK)
        upd = jnp.dot(a, wo_ref[pl.ds(row, FF_CHUNK), :], preferred_element_type=F32)
        if init:
            acc_ref[...] = upd
        else:
            acc_ref[...] += upd

    first_matmul(0, 0)
    first_matmul(1, 1)
    gate_and_second_matmul(0, 0, init=True)

    def pair(i, carry):
        c = 2 * i + 1
        first_matmul(c + 1, 0)
        gate_and_second_matmul(c, 1)
        first_matmul(c + 2, 1)
        gate_and_second_matmul(c + 1, 0)
        return carry

    lax.fori_loop(0, (n_c - 3) // 2, pair, 0)
    first_matmul(n_c - 1, 0)
    gate_and_second_matmul(n_c - 2, 1)
    gate_and_second_matmul(n_c - 1, 0)
    for p in range(PHASES):
        out = acc_ref[p * PHASE_ROWS:(p + 1) * PHASE_ROWS, :] + bo_ref[...]
        y = _residual_layer_norm(_slab_read_phase(slab_ref, p), out, _mod(m, 5), g_ref[...], beta_ref[...])
        _slab_write_phase(slab_ref, p, y)
    _slab_drain(slab_ref, o_ref)


def _ffn(x, mods, W, layer, seq_len, per_batch):
    t = x.shape[0]
    assert (D_FF // FF_CHUNK) % 2 == 1
    row = pl.BlockSpec((ROW_TILE, D_MODEL), lambda i: (i, 0))
    names = ["ff_in_w", "ff_in_b", "ff_conv_w", "ff_conv_b", "ff_out_w", "ff_out_b"]
    return pl.pallas_call(
        functools.partial(_ffn_kernel, seq_len, per_batch),
        grid=(t // ROW_TILE,),
        in_specs=[row, _mods_spec(layer, per_batch)] + [_layer_block(W[n], layer) for n in names]
                 + [_layer_block(W["ln_g"], layer, 1), _layer_block(W["ln_b"], layer, 1)],
        out_specs=row,
        out_shape=jax.ShapeDtypeStruct((t, D_MODEL), F32),
        scratch_shapes=[pltpu.VMEM((ROW_TILE, D_MODEL), F32), pltpu.VMEM((ROW_TILE, D_MODEL), BF16),
                        pltpu.VMEM((2, 2, ROW_TILE, FF_CHUNK), F32), _slab_scratch()],
        compiler_params=_params(1),
        name="conv_ffn",
    )(x, mods, *[W[n] for n in names], W["ln_g"], W["ln_b"])


def _qkv_kernel(rope, per_batch, x_ref, mod_ref, w_ref, b_ref, qg_ref, kg_ref, *rest):
    scale = LOG2_E * HEAD_DIM ** -0.5
    h = _modulate(x_ref[...], _mod_row(mod_ref, per_batch), 0).astype(BF16)
    z = jnp.dot(h, w_ref[...], preferred_element_type=F32) + b_ref[...]
    if rope:
        ws_ref, bs_ref, qgs_ref, kgs_ref, cos_ref, sin_ref, q_ref, k_ref, v_ref = rest
        zp = jnp.dot(h, ws_ref[...], preferred_element_type=F32) + bs_ref[...]
        cos, sin = cos_ref[...], sin_ref[...]
        tables = {"q": (qg_ref[...] * scale * cos, qgs_ref[...] * scale * sin),
                  "k": (kg_ref[...] * cos, kgs_ref[...] * sin)}
    else:
        q_ref, k_ref, v_ref = rest

    def head(col, kind):
        zi = z[:, col:col + HEAD_DIM]
        inv = lax.rsqrt(jnp.mean(zi * zi, axis=-1, keepdims=True) + QK_EPS)
        if rope:
            gc, gs = tables[kind]
            return inv * (zi * gc + zp[:, col:col + HEAD_DIM] * gs)
        if kind == "q":
            return zi * inv * (qg_ref[...] * scale)
        return zi * inv * kg_ref[...]

    for i in range(N_HEADS):
        q_ref[:, i * HEAD_DIM:(i + 1) * HEAD_DIM] = head(i * HEAD_DIM, "q").astype(q_ref.dtype)
    for i in range(N_KV_HEADS):
        cols = slice(i * HEAD_DIM, (i + 1) * HEAD_DIM)
        k_ref[:, cols] = head(Q_DIM + i * HEAD_DIM, "k").astype(k_ref.dtype)
        v_ref[:, cols] = z[:, QK_DIM + i * HEAD_DIM:QK_DIM + (i + 1) * HEAD_DIM].astype(v_ref.dtype)


def _qkv(x, mods, W, j, layer, rope_tables, kv_dtype, per_batch):
    t = x.shape[0]
    rope = rope_tables is not None
    row = pl.BlockSpec((ROW_TILE, D_MODEL), lambda i: (i, 0))
    kv_row = pl.BlockSpec((ROW_TILE, KV_DIM), lambda i: (i, 0))
    names = ["at_qkv_w", "at_qkv_b", "at_q_gain", "at_k_gain"]
    if rope:
        names += ["at_qk_w_swap", "at_qk_b_swap", "at_q_gain_swap", "at_k_gain_swap"]
    in_specs = [row, _mods_spec(layer, per_batch)] + [_layer_block(W[n], j) for n in names]
    args = [x, mods] + [W[n] for n in names]
    if rope:
        in_specs += [_resident((ROW_TILE, HEAD_DIM))] * 2
        args += list(rope_tables)
    return pl.pallas_call(
        functools.partial(_qkv_kernel, rope, per_batch),
        grid=(t // ROW_TILE,),
        in_specs=in_specs,
        out_specs=[row, kv_row, kv_row],
        out_shape=[jax.ShapeDtypeStruct((t, Q_DIM), BF16), jax.ShapeDtypeStruct((t, KV_DIM), kv_dtype),
                   jax.ShapeDtypeStruct((t, KV_DIM), kv_dtype)],
        compiler_params=_params(1),
        name="qkv_rope" if rope else "qkv",
    )(*args)


def _attn_kernel(has_ctx, seq_len, q_ref, k_ref, v_ref, *rest):
    if has_ctx:
        kc_ref, vc_ref, o_ref = rest
        kc = kc_ref[...]
        vc = jnp.concatenate([vc_ref[...], jnp.ones(vc_ref.shape, BF16)], axis=1)
    else:
        (o_ref,) = rest
    nt = (((1,), (1,)), ((), ()))
    n_seq = k_ref.shape[0] // seq_len
    q_rows = q_ref.shape[0] // n_seq
    ones = jnp.ones((seq_len, HEAD_DIM), BF16)
    for s in range(n_seq):
        k = k_ref[s * seq_len:(s + 1) * seq_len, :].astype(BF16)
        v = jnp.concatenate([v_ref[s * seq_len:(s + 1) * seq_len, :].astype(BF16), ones], axis=1)
        rows = slice(s * q_rows, (s + 1) * q_rows)
        for g0 in range(0, GROUP, HEAD_STACK):
            heads = range(g0, g0 + HEAD_STACK)
            q = jnp.concatenate([q_ref[rows, g * HEAD_DIM:(g + 1) * HEAD_DIM] for g in heads], axis=0)
            sc_new = lax.dot_general(q, k, nt, preferred_element_type=F32)
            mx = jnp.max(sc_new, axis=-1, keepdims=True)
            if has_ctx:
                sc_ctx = lax.dot_general(q, kc, nt, preferred_element_type=F32)
                mx = jnp.maximum(mx, jnp.max(sc_ctx, axis=-1, keepdims=True))
            o = jnp.dot(jnp.exp2(sc_new - mx).astype(BF16), v, preferred_element_type=F32)
            if has_ctx:
                o = o + jnp.dot(jnp.exp2(sc_ctx - mx).astype(BF16), vc, preferred_element_type=F32)
            o = (o[:, :HEAD_DIM] / o[:, HEAD_DIM:]).astype(BF16)
            for n, g in enumerate(heads):
                o_ref[rows, g * HEAD_DIM:(g + 1) * HEAD_DIM] = o[n * q_rows:(n + 1) * q_rows]


def _attention(q, k, v, ctx, seq_len, q_tile):
    t = q.shape[0]
    gw = GROUP * HEAD_DIM
    kv_rows = max(seq_len, q_tile)
    n_b = t // kv_rows
    n_q = kv_rows // q_tile
    q_spec = pl.BlockSpec((q_tile, gw), lambda b, h, i: (b * n_q + i, h))
    kv_spec = pl.BlockSpec((kv_rows, HEAD_DIM), lambda b, h, i: (b, h))
    in_specs = [q_spec, kv_spec, kv_spec]
    args = [q, k, v]
    if ctx is not None:
        kc, vc, layer = ctx
        ctx_spec = pl.BlockSpec((None, None, kc.shape[2], HEAD_DIM), lambda b, h, i: (b, layer, 0, h))
        in_specs += [ctx_spec, ctx_spec]
        args += [kc, vc]
    return pl.pallas_call(
        functools.partial(_attn_kernel, ctx is not None, seq_len),
        grid=(n_b, N_KV_HEADS, n_q),
        in_specs=in_specs,
        out_specs=q_spec,
        out_shape=jax.ShapeDtypeStruct((t, Q_DIM), BF16),
        compiler_params=_params(3),
        name="attention_ctx" if ctx is not None else "attention",
    )(*args)


def _hyena_feats(seq_len):
    t = np.linspace(0.0, 1.0, seq_len)[:, None]
    n_bands = (POS_EMB_DIM - 1) // 2
    w = 2.0 * math.pi * np.arange(seq_len) / seq_len
    f = np.linspace(1e-4, n_bands - 1, n_bands)
    ang = w[:, None] * f[None, :]
    z = np.concatenate([t, np.cos(ang), -np.sin(ang)], -1)
    z = np.pad(z, ((0, 0), (0, FILTER_PAD - POS_EMB_DIM)))
    max_decay = math.log(DECAY_TARGET) / FAST_DECAY_PCT
    min_decay = math.log(DECAY_TARGET) / SLOW_DECAY_PCT
    deltas = np.abs(np.linspace(min_decay, max_decay, D_MODEL))[None, :]
    return z.astype(np.float32), t.astype(np.float32), deltas.astype(np.float32)


def _dft_mats(seq_len):
    n = 2 * seq_len
    idx = np.arange(seq_len)
    ang = 2.0 * math.pi * ((idx[:, None] * idx[None, :]) % n) / n
    sign = np.where(idx % 2 == 0, 1.0, -1.0)
    fa, fb = np.cos(ang), -np.sin(ang)
    fb[0, :] = sign
    fm = np.concatenate([fa, fb], 0)
    ga, gb = (2.0 / n) * np.cos(ang), -(2.0 / n) * np.sin(ang)
    ga[:, 0] = 1.0 / n
    gb[:, 0] = sign / n
    time_of = (np.arange(seq_len) % (seq_len // PHASES)) * PHASES + np.arange(seq_len) // (seq_len // PHASES)
    return (fm.astype(np.float32), fm[:, time_of].astype(np.float32),
            ga[time_of, :].astype(np.float32), gb[time_of, :].astype(np.float32))


def _rope_tables(seq_len):
    rows = np.repeat(np.arange(seq_len // GRID_W), GRID_W).astype(np.float64)
    cols = np.tile(np.arange(GRID_W), seq_len // GRID_W).astype(np.float64)
    half = HEAD_DIM // 2
    inv = ROPE_THETA ** (-np.arange(0, half, 2, dtype=np.float64) / half)
    ang = np.concatenate([rows[:, None] * inv, cols[:, None] * inv], -1)
    cos = np.repeat(np.cos(ang), 2, axis=-1)
    sin = np.stack([-np.sin(ang), np.sin(ang)], axis=-1).reshape(seq_len, HEAD_DIM)
    return cos.astype(np.float32), sin.astype(np.float32)


def _pair_swap_matrix(n):
    p = np.zeros((n, n), np.float32)
    idx = np.arange(n)
    p[idx ^ 1, idx] = 1.0
    return p


def _trunk(x, mods, seq_len, per_batch, W, rope, ctx):
    fm, fm_pm, ga_pm, gb_pm = (jnp.asarray(a).astype(BF16) for a in _dft_mats(seq_len))
    feats, t_col, deltas = (jnp.asarray(a) for a in _hyena_feats(seq_len))
    new_k, new_v = [], []
    for l in range(DEPTH):
        j = l // 2
        if l % 2 == 0:
            ha, hb, hc = _hyena_filter((feats, t_col, deltas, fm), W, j, seq_len)
            x0, u = _hyena_in(x, mods, W, j, l, seq_len, per_batch)
            a = _dft_conv(u, x0, ha, hb, hc, fm_pm, ga_pm, gb_pm, seq_len)
            w_o, b_o = W["hy_out_w"], W["hy_out_b"]
        else:
            if ctx is None:
                q, k, v = _qkv(x, mods, W, j, l, None, F32, per_batch)
                new_k.append(k)
                new_v.append(v)
                a = _attention(q, k, v, None, seq_len, ROW_TILE)
            else:
                q, k, v = _qkv(x, mods, W, j, l, rope, BF16, per_batch)
                a = _attention(q, k, v, (ctx[0], ctx[1], j), seq_len, CTX_Q_TILE)
            w_o, b_o = W["at_o_w"], W["at_o_b"]
        x = _proj_ln(a, x, mods, w_o, b_o, j, W, l, per_batch, l % 2 == 0)
        x = _ffn(x, mods, W, l, seq_len, per_batch)
    return x, new_k, new_v


def kernel(x_prompt, x_sample, cache_k, cache_v, c, c_ctx, w_mod, b_mod, ln_g, ln_b, hy_in_w, hy_in_b, hy_short_w, hy_short_b, hy_pos_w1, hy_pos_b1, hy_pos_w2, hy_pos_b2, hy_pos_wout, hy_freq, hy_filt_bias, hy_out_w, hy_out_b, at_qkv_w, at_qkv_b, at_q_gain, at_k_gain, at_o_w, at_o_b, ff_in_w, ff_in_b, ff_conv_w, ff_conv_b, ff_out_w, ff_out_b):
    batch, seq, _ = x_prompt.shape
    dec_batch, dec_seq, _ = x_sample.shape
    assert dec_seq == ROW_TILE and ROW_TILE % seq == 0 and 1 + dec_batch <= COND_ROWS
    n_attn = at_qkv_w.shape[0]
    pad = FILTER_PAD - FILTER_WIDTH

    def vec(a):
        return a[:, None, :]

    qkv_w = at_qkv_w.astype(BF16)
    swap = jnp.asarray(_pair_swap_matrix(QK_DIM))
    swap_head = jnp.asarray(_pair_swap_matrix(HEAD_DIM))
    hi = lax.Precision.HIGHEST
    W = {
        "ln_g": ln_g[:, :, None, :], "ln_b": ln_b[:, :, None, :],
        "hy_in_w": hy_in_w.astype(BF16), "hy_in_b": vec(hy_in_b),
        "hy_short_w": hy_short_w, "hy_short_b": vec(hy_short_b),
        "hy_pos_w1": jnp.pad(hy_pos_w1, ((0, 0), (0, FILTER_PAD - POS_EMB_DIM), (0, pad))),
        "hy_pos_b1": vec(jnp.pad(hy_pos_b1, ((0, 0), (0, pad)))),
        "hy_pos_w2": jnp.pad(hy_pos_w2, ((0, 0), (0, 0), (0, pad), (0, pad))),
        "hy_pos_b2": jnp.pad(hy_pos_b2, ((0, 0), (0, 0), (0, pad)))[:, :, None, :],
        "hy_freq": vec(jnp.pad(hy_freq, ((0, 0), (0, pad)))),
        "hy_pos_wout": hy_pos_wout, "hy_filt_bias": vec(hy_filt_bias),
        "hy_out_w": hy_out_w.astype(BF16), "hy_out_b": vec(hy_out_b),
        "at_qkv_w": qkv_w, "at_qkv_b": vec(at_qkv_b),
        "at_q_gain": vec(at_q_gain), "at_k_gain": vec(at_k_gain),
        "at_qk_w_swap": jnp.dot(qkv_w[:, :, :QK_DIM], swap.astype(BF16), preferred_element_type=F32).astype(BF16),
        "at_qk_b_swap": vec(jnp.dot(at_qkv_b[:, :QK_DIM], swap, precision=hi)),
        "at_q_gain_swap": vec(jnp.dot(at_q_gain, swap_head, precision=hi)),
        "at_k_gain_swap": vec(jnp.dot(at_k_gain, swap_head, precision=hi)),
        "at_o_w": at_o_w.astype(BF16), "at_o_b": vec(at_o_b),
        "ff_in_w": ff_in_w.astype(BF16), "ff_in_b": vec(ff_in_b),
        "ff_conv_w": ff_conv_w, "ff_conv_b": vec(ff_conv_b),
        "ff_out_w": ff_out_w.astype(BF16), "ff_out_b": vec(ff_out_b),
    }
    rope = tuple(jnp.asarray(a) for a in _rope_tables(dec_seq))

    cond = jnp.concatenate([c_ctx[None, :], c, jnp.zeros((COND_ROWS - 1 - dec_batch, D_MODEL), F32)], 0)
    mods = _modulation_all(cond, w_mod, b_mod)

    past = cache_k.shape[2]
    ctx_k = cache_k.astype(BF16).reshape(dec_batch, n_attn, past, KV_DIM)
    ctx_v = cache_v.astype(BF16).reshape(dec_batch, n_attn, past, KV_DIM)

    y_p, new_k, new_v = _trunk(x_prompt.reshape(batch * seq, D_MODEL), mods, seq, False, W, None, None)
    y_s, _, _ = _trunk(x_sample.reshape(dec_batch * dec_seq, D_MODEL), mods, dec_seq, True, W, rope, (ctx_k, ctx_v))

    kv_shape = (batch, seq, N_KV_HEADS, HEAD_DIM)
    new_cache_k = jnp.stack([k.reshape(kv_shape) for k in new_k], axis=1)
    new_cache_v = jnp.stack([v.reshape(kv_shape) for v in new_v], axis=1)
    return (y_p.reshape(batch, seq, D_MODEL), y_s.reshape(dec_batch, dec_seq, D_MODEL), new_cache_k, new_cache_v)
```

```python
import functools
import math

import jax
import jax.numpy as jnp
import numpy as np
from jax import lax
from jax.experimental import pallas as pl
from jax.experimental.pallas import tpu as pltpu

D_MODEL = 1024
DEPTH = 4
GRID_W = 64
N_HEADS = 8
N_KV_HEADS = 2
HEAD_DIM = 128
GROUP = N_HEADS // N_KV_HEADS
Q_DIM = N_HEADS * HEAD_DIM
KV_DIM = N_KV_HEADS * HEAD_DIM
QK_DIM = Q_DIM + KV_DIM
QKV_DIM = Q_DIM + 2 * KV_DIM
ROPE_THETA = 10000.0
QK_EPS = 1e-6
POS_EMB_DIM = 33
FILTER_WIDTH = 64
N_INNER_MLPS = 2
FAST_DECAY_PCT = 0.3
SLOW_DECAY_PCT = 1.5
DECAY_TARGET = 1e-2
MOD_SHIFT = 0.0
D_FF = 2816
LN_EPS = 1e-5
N_MOD = 6
DN_ALPHA = (2 * DEPTH) ** 0.25

F32 = jnp.float32
BF16 = jnp.bfloat16

LANES = 128
SUBLANES = 8
MXU_WIDTH = 256
ROW_TILE = 1024
FF_CHUNK = MXU_WIDTH
HY_CHUNK = MXU_WIDTH
PHASES = 4
PHASE_ROWS = ROW_TILE // PHASES
GELU_K = math.sqrt(2.0 / math.pi)
LOG2_E = math.log2(math.e)
HEAD_STACK = 1
CTX_Q_TILE = 512
MOD_COL_TILE = 1536
COND_ROWS = 16
FILTER_PAD = LANES
VMEM_LIMIT = 56 * 1024 * 1024


def _params(n_axes):
    return pltpu.CompilerParams(dimension_semantics=("arbitrary",) * n_axes,
                                vmem_limit_bytes=VMEM_LIMIT)


def _resident(shape):
    nd = len(shape)
    return pl.BlockSpec(shape, lambda *_: (0,) * nd, pipeline_mode=pl.Buffered(1))


def _layer_block(arr, *lead):
    shape = arr.shape[len(lead):]
    idx = tuple(lead) + (0,) * len(shape)
    return pl.BlockSpec((None,) * len(lead) + shape, lambda *_: idx, pipeline_mode=pl.Buffered(1))


def _mods_spec(layer, per_batch):
    if per_batch:
        return pl.BlockSpec((None, SUBLANES, N_MOD * D_MODEL), lambda i: (layer, (i + 1) // SUBLANES, 0))
    return pl.BlockSpec((None, SUBLANES, N_MOD * D_MODEL), lambda i: (layer, 0, 0))


def _mod_row(mod_ref, per_batch):
    row = (pl.program_id(0) + 1) % SUBLANES if per_batch else 0
    return mod_ref[pl.ds(row, 1), :]


def _mod(m, idx):
    return m[:, idx * D_MODEL:(idx + 1) * D_MODEL]


def _modulate(x, m, shift_idx):
    return x * (1.0 + _mod(m, shift_idx + 1)) + _mod(m, shift_idx)


def _residual_layer_norm(x, out, gate, g, b):
    y = DN_ALPHA * x + (1.0 + gate) * out
    mu = jnp.mean(y, axis=-1, keepdims=True)
    yc = y - mu
    var = jnp.mean(yc * yc, axis=-1, keepdims=True)
    return yc * lax.rsqrt(var + LN_EPS) * g + b


def _column_slab_specs():
    return [pl.BlockSpec((ROW_TILE, LANES), functools.partial(lambda j, i: (i, j), j)) for j in range(D_MODEL // LANES)]


def _read_phase(x_refs, p):
    return jnp.concatenate([r[pl.ds(p, PHASE_ROWS, stride=PHASES), :] for r in x_refs], axis=1)


def _slab_drain(slab_ref, o_ref):
    for j in range(D_MODEL // LANES):
        o_ref[:, j * LANES:(j + 1) * LANES] = slab_ref[j]


def _slab_write_phase(slab_ref, p, y):
    for j in range(D_MODEL // LANES):
        slab_ref[j, pl.ds(p, PHASE_ROWS, stride=PHASES), :] = y[:, j * LANES:(j + 1) * LANES]


def _slab_scratch():
    return pltpu.VMEM((D_MODEL // LANES, ROW_TILE, LANES), F32)


def _phase_bounds(seq_len):
    per_seq = seq_len // PHASES
    pos = lax.broadcasted_iota(jnp.int32, (PHASE_ROWS, 1), 0) & (per_seq - 1)
    return pos == 0, pos == per_seq - 1


def _phase_conv3(z, w, bias, conv_bias, seq_start, seq_end, scale=1.0):
    w = w * scale
    b_all = bias * (w[0:1] + w[1:2] + w[2:3]) + conv_bias * scale
    blk = [z[p * PHASE_ROWS:(p + 1) * PHASE_ROWS] for p in range(PHASES)]
    prev0 = jnp.where(seq_start, -bias, pltpu.roll(blk[PHASES - 1], 1, 0))
    next_last = jnp.where(seq_end, -bias, pltpu.roll(blk[0], PHASE_ROWS - 1, 0))
    out = []
    for p in range(PHASES):
        zp = prev0 if p == 0 else blk[p - 1]
        zn = next_last if p == PHASES - 1 else blk[p + 1]
        out.append(zp * w[0:1] + blk[p] * w[1:2] + zn * w[2:3] + b_all)
    return out


def _mod_kernel(cond_ref, w_ref, b_ref, o_ref):
    c = cond_ref[...]
    s = (c * jax.nn.sigmoid(c)).astype(BF16)
    o_ref[0] = jnp.dot(s, w_ref[0].astype(BF16), preferred_element_type=F32) + b_ref[0]


def _modulation_all(cond, w_mod, b_mod):
    n_col = (N_MOD * D_MODEL) // MOD_COL_TILE
    return pl.pallas_call(
        _mod_kernel,
        grid=(DEPTH, n_col),
        in_specs=[pl.BlockSpec((COND_ROWS, D_MODEL), lambda l, j: (0, 0)),
                  pl.BlockSpec((1, D_MODEL, MOD_COL_TILE), lambda l, j: (l, 0, j)),
                  pl.BlockSpec((1, 1, MOD_COL_TILE), lambda l, j: (l, 0, j))],
        out_specs=pl.BlockSpec((1, COND_ROWS, MOD_COL_TILE), lambda l, j: (l, 0, j)),
        out_shape=jax.ShapeDtypeStruct((DEPTH, COND_ROWS, N_MOD * D_MODEL), F32),
        compiler_params=_params(2),
        name="modulation",
    )(cond, w_mod, b_mod.reshape(DEPTH, 1, N_MOD * D_MODEL))


def _hyena_in_kernel(seq_len, per_batch, *refs):
    n_slab = D_MODEL // LANES
    x_refs, (mod_ref, w_ref, b_ref, sw_ref, sb_ref, x0_ref, u_ref, h_ref) = refs[:n_slab], refs[n_slab:]
    m = _mod_row(mod_ref, per_batch)
    for p in range(PHASES):
        h_ref[p * PHASE_ROWS:(p + 1) * PHASE_ROWS, :] = _modulate(_read_phase(x_refs, p), m, 0).astype(BF16)
    seq_start, seq_end = _phase_bounds(seq_len)
    for c in range(D_MODEL // HY_CHUNK):
        conv = []
        for part in range(3):
            cs = slice(part * D_MODEL + c * HY_CHUNK, part * D_MODEL + (c + 1) * HY_CHUNK)
            z = jnp.dot(h_ref[...], w_ref[:, cs], preferred_element_type=F32)
            conv.append(_phase_conv3(z, sw_ref[:, cs], b_ref[:, cs], sb_ref[:, cs], seq_start, seq_end))
        cs = slice(c * HY_CHUNK, (c + 1) * HY_CHUNK)
        for p, (x0, x1, v) in enumerate(zip(*conv)):
            rs = slice(p * PHASE_ROWS, (p + 1) * PHASE_ROWS)
            x0_ref[rs, cs] = x0.astype(BF16)
            u_ref[rs, cs] = (v * x1).astype(BF16)


def _hyena_in(x, mods, W, j, layer, seq_len, per_batch):
    t = x.shape[0]
    row = pl.BlockSpec((ROW_TILE, D_MODEL), lambda i: (i, 0))
    return pl.pallas_call(
        functools.partial(_hyena_in_kernel, seq_len, per_batch),
        grid=(t // ROW_TILE,),
        in_specs=_column_slab_specs() + [
            _mods_spec(layer, per_batch),
            _layer_block(W["hy_in_w"], j), _layer_block(W["hy_in_b"], j),
            _layer_block(W["hy_short_w"], j), _layer_block(W["hy_short_b"], j)],
        out_specs=[row, row],
        out_shape=[jax.ShapeDtypeStruct((t, D_MODEL), BF16)] * 2,
        scratch_shapes=[pltpu.VMEM((ROW_TILE, D_MODEL), BF16)],
        compiler_params=_params(1),
        name="hyena_in",
    )(*([x] * (D_MODEL // LANES)), mods, W["hy_in_w"], W["hy_in_b"], W["hy_short_w"], W["hy_short_b"])


def _hyena_filter_kernel(seq_len, z_ref, w1_ref, b1_ref, w2_ref, b2_ref, freq_ref, wo0_ref, wo1_ref,
                         t_ref, delta_ref, bias_ref, fm_ref, ha_ref, hb_ref, hc_ref, h_ref):
    hi = lax.Precision.HIGHEST

    @pl.when(pl.program_id(0) == 0)
    def _():
        freq = freq_ref[...]
        h = jnp.sin(freq * (jnp.dot(z_ref[...], w1_ref[...], precision=hi, preferred_element_type=F32)
                            + b1_ref[...]))
        for i in range(N_INNER_MLPS):
            h = jnp.sin(freq * (jnp.dot(h, w2_ref[i], precision=hi, preferred_element_type=F32) + b2_ref[i]))
        h_ref[...] = h

    h = h_ref[...]
    decay = jnp.exp(-t_ref[...] * delta_ref[...]) + MOD_SHIFT
    zero_rows = jnp.zeros((FILTER_PAD - FILTER_WIDTH, wo0_ref.shape[1]), F32)

    def taps(wo_ref):
        wo = jnp.concatenate([wo_ref[...], zero_rows], axis=0)
        return jnp.dot(h, wo, precision=hi, preferred_element_type=F32) * decay

    k0, k1 = taps(wo0_ref), taps(wo1_ref)
    row0 = lax.broadcasted_iota(jnp.int32, (seq_len, 1), 0) == 0
    k0 = jnp.where(row0, k0 + bias_ref[...], k0)
    k1 = jnp.where(row0, 0.0, k1)
    fm = fm_ref[...]
    s0 = jnp.dot(fm, k0.astype(BF16), preferred_element_type=F32)
    s1 = jnp.dot(fm, k1.astype(BF16), preferred_element_type=F32)
    ha = s0[:seq_len] + s1[:seq_len]
    s0b, s1b = s0[seq_len:], s1[seq_len:]
    nyq = s0b + s1b
    ha_ref[...] = ha
    hb_ref[...] = jnp.where(row0, 0.0, s0b - s1b)
    hc_ref[...] = jnp.where(row0, nyq, ha)


def _hyena_filter(consts, W, j, seq_len):
    feats, t_col, deltas, fm = consts
    dt = MXU_WIDTH
    n_d = D_MODEL // dt
    out = pl.BlockSpec((seq_len, dt), lambda i: (0, i))
    wout = W["hy_pos_wout"]
    return pl.pallas_call(
        functools.partial(_hyena_filter_kernel, seq_len),
        grid=(n_d,),
        in_specs=[_resident((seq_len, FILTER_PAD)),
                  _layer_block(W["hy_pos_w1"], j), _layer_block(W["hy_pos_b1"], j),
                  _layer_block(W["hy_pos_w2"], j), _layer_block(W["hy_pos_b2"], j),
                  _layer_block(W["hy_freq"], j),
                  pl.BlockSpec((None, FILTER_WIDTH, dt), lambda i: (j, 0, i)),
                  pl.BlockSpec((None, FILTER_WIDTH, dt), lambda i: (j, 0, n_d + i)),
                  _resident((seq_len, 1)), pl.BlockSpec((1, dt), lambda i: (0, i)),
                  pl.BlockSpec((None, 1, dt), lambda i: (j, 0, i)),
                  _resident((2 * seq_len, seq_len))],
        out_specs=[out, out, out],
        out_shape=[jax.ShapeDtypeStruct((seq_len, D_MODEL), F32)] * 3,
        scratch_shapes=[pltpu.VMEM((seq_len, FILTER_PAD), F32)],
        compiler_params=_params(1),
        name="hyena_filter",
    )(feats, W["hy_pos_w1"], W["hy_pos_b1"], W["hy_pos_w2"], W["hy_pos_b2"], W["hy_freq"], wout, wout,
      t_col, deltas, W["hy_filt_bias"], fm)


def _dft_conv_kernel(seq_len, u_ref, x0_ref, ha_ref, hb_ref, hc_ref, fm_ref, ga_ref, gb_ref, o_ref):
    ha, hb, hc = ha_ref[...], hb_ref[...], hc_ref[...]
    per = seq_len // PHASES
    for s in range(ROW_TILE // seq_len):
        rows = [slice(p * PHASE_ROWS + s * per, p * PHASE_ROWS + (s + 1) * per) for p in range(PHASES)]
        u = jnp.concatenate([u_ref[r, :] for r in rows], axis=0)
        spec = jnp.dot(fm_ref[...], u, preferred_element_type=F32)
        a, b = spec[:seq_len], spec[seq_len:]
        ya = (a * ha - b * hb).astype(BF16)
        yb = (a * hb + b * hc).astype(BF16)
        y = (jnp.dot(ga_ref[...], ya, preferred_element_type=F32)
             + jnp.dot(gb_ref[...], yb, preferred_element_type=F32))
        for p, r in enumerate(rows):
            o_ref[r, :] = (x0_ref[r, :].astype(F32) * y[p * per:(p + 1) * per]).astype(BF16)


def _dft_conv(u, x0, ha, hb, hc, fm, ga, gb, seq_len):
    t = u.shape[0]
    dt = 512 if seq_len == ROW_TILE else D_MODEL
    row = pl.BlockSpec((ROW_TILE, dt), lambda j, i: (i, j))
    filt = pl.BlockSpec((seq_len, dt), lambda j, i: (0, j))
    return pl.pallas_call(
        functools.partial(_dft_conv_kernel, seq_len),
        grid=(D_MODEL // dt, t // ROW_TILE),
        in_specs=[row, row, filt, filt, filt,
                  _resident((2 * seq_len, seq_len)), _resident((seq_len, seq_len)),
                  _resident((seq_len, seq_len))],
        out_specs=row,
        out_shape=jax.ShapeDtypeStruct((t, D_MODEL), BF16),
        compiler_params=_params(2),
        name="hyena_dft_conv",
    )(u, x0, ha, hb, hc, fm, ga, gb)


def _proj_ln_kernel(per_batch, phase_major, a_ref, x_ref, mod_ref, w_ref, b_ref, g_ref, beta_ref, o_ref, *scratch):
    out = jnp.dot(a_ref[...], w_ref[...], preferred_element_type=F32) + b_ref[...]
    if phase_major:
        (slab_ref,) = scratch
        for p in range(PHASES):
            _slab_write_phase(slab_ref, p, out[p * PHASE_ROWS:(p + 1) * PHASE_ROWS])
        out = jnp.concatenate([slab_ref[j] for j in range(D_MODEL // LANES)], axis=1)
    gate = _mod(_mod_row(mod_ref, per_batch), 2)
    o_ref[...] = _residual_layer_norm(x_ref[...], out, gate, g_ref[...], beta_ref[...])


def _proj_ln(a, x, mods, w, b, j, W, layer, per_batch, phase_major):
    t = x.shape[0]
    row = pl.BlockSpec((ROW_TILE, D_MODEL), lambda i: (i, 0))
    return pl.pallas_call(
        functools.partial(_proj_ln_kernel, per_batch, phase_major),
        grid=(t // ROW_TILE,),
        in_specs=[row, row, _mods_spec(layer, per_batch), _layer_block(w, j), _layer_block(b, j),
                  _layer_block(W["ln_g"], layer, 0), _layer_block(W["ln_b"], layer, 0)],
        out_specs=row,
        out_shape=jax.ShapeDtypeStruct((t, D_MODEL), F32),
        scratch_shapes=[_slab_scratch()] if phase_major else [],
        compiler_params=_params(1),
        name="proj_ln_phase" if phase_major else "proj_ln",
    )(a, x, mods, w, b, W["ln_g"], W["ln_b"])


def _ffn_kernel(seq_len, per_batch, *refs):
    n_slab = D_MODEL // LANES
    x_refs = refs[:n_slab]
    (mod_ref, wi_ref, bi_ref, cw_ref, cb_ref, wo_ref, bo_ref, g_ref, beta_ref,
     o_ref, acc_ref, h_ref, z_ref, slab_ref) = refs[n_slab:]
    m = _mod_row(mod_ref, per_batch)
    for p in range(PHASES):
        h_ref[p * PHASE_ROWS:(p + 1) * PHASE_ROWS, :] = _modulate(_read_phase(x_refs, p), m, 3).astype(BF16)
    seq_start, seq_end = _phase_bounds(seq_len)
    n_c = D_FF // FF_CHUNK

    def cols(c, part):
        return pl.ds(pl.multiple_of(part * D_FF + c * FF_CHUNK, FF_CHUNK), FF_CHUNK)

    def first_matmul(c, slot):
        for part in range(2):
            z_ref[slot, part] = jnp.dot(h_ref[...], wi_ref[:, cols(c, part)], preferred_element_type=F32)

    def gate_and_second_matmul(c, slot, init=False):
        conv = [_phase_conv3(z_ref[slot, part], cw_ref[:, cols(c, part)], bi_ref[:, cols(c, part)],
                             cb_ref[:, cols(c, part)], seq_start, seq_end, scale=(1.0, 0.5)[part])
                for part in range(2)]
        a = []
        for g, half_v in zip(*conv):
            t = jnp.tanh(g * (GELU_K + (GELU_K * 0.044715) * (g * g)))
            gv = g * half_v
            a.append((gv + gv * t).astype(BF16))
        a = jnp.concatenate(a, axis=0)
        row = pl.multiple_of(c * FF_CHUNK, FF_CHUNK)
        upd = jnp.dot(a, wo_ref[pl.ds(row, FF_CHUNK), :], preferred_element_type=F32)
        if init:
            acc_ref[...] = upd
        else:
            acc_ref[...] += upd

    first_matmul(0, 0)
    first_matmul(1, 1)
    gate_and_second_matmul(0, 0, init=True)

    def pair(i, carry):
        c = 2 * i + 1
        first_matmul(c + 1, 0)
        gate_and_second_matmul(c, 1)
        first_matmul(c + 2, 1)
        gate_and_second_matmul(c + 1, 0)
        return carry

    lax.fori_loop(0, (n_c - 3) // 2, pair, 0)
    first_matmul(n_c - 1, 0)
    gate_and_second_matmul(n_c - 2, 1)
    gate_and_second_matmul(n_c - 1, 0)
    for p in range(PHASES):
        out = acc_ref[p * PHASE_ROWS:(p + 1) * PHASE_ROWS, :] + bo_ref[...]
        y = _residual_layer_norm(_read_phase(x_refs, p), out, _mod(m, 5), g_ref[...], beta_ref[...])
        _slab_write_phase(slab_ref, p, y)
    _slab_drain(slab_ref, o_ref)


def _ffn(x, mods, W, layer, seq_len, per_batch):
    t = x.shape[0]
    assert (D_FF // FF_CHUNK) % 2 == 1
    row = pl.BlockSpec((ROW_TILE, D_MODEL), lambda i: (i, 0))
    names = ["ff_in_w", "ff_in_b", "ff_conv_w", "ff_conv_b", "ff_out_w", "ff_out_b"]
    return pl.pallas_call(
        functools.partial(_ffn_kernel, seq_len, per_batch),
        grid=(t // ROW_TILE,),
        in_specs=_column_slab_specs() + [_mods_spec(layer, per_batch)] + [_layer_block(W[n], layer) for n in names]
                 + [_layer_block(W["ln_g"], layer, 1), _layer_block(W["ln_b"], layer, 1)],
        out_specs=row,
        out_shape=jax.ShapeDtypeStruct((t, D_MODEL), F32),
        scratch_shapes=[pltpu.VMEM((ROW_TILE, D_MODEL), F32), pltpu.VMEM((ROW_TILE, D_MODEL), BF16),
                        pltpu.VMEM((2, 2, ROW_TILE, FF_CHUNK), F32), _slab_scratch()],
        compiler_params=_params(1),
        name="conv_ffn",
    )(*([x] * (D_MODEL // LANES)), mods, *[W[n] for n in names], W["ln_g"], W["ln_b"])


def _qkv_kernel(rope, per_batch, x_ref, mod_ref, w_ref, b_ref, qg_ref, kg_ref, *rest):
    scale = LOG2_E * HEAD_DIM ** -0.5
    h = _modulate(x_ref[...], _mod_row(mod_ref, per_batch), 0).astype(BF16)
    z = jnp.dot(h, w_ref[...], preferred_element_type=F32) + b_ref[...]
    if rope:
        ws_ref, bs_ref, qgs_ref, kgs_ref, cos_ref, sin_ref, q_ref, k_ref, v_ref = rest
        zp = jnp.dot(h, ws_ref[...], preferred_element_type=F32) + bs_ref[...]
        cos, sin = cos_ref[...], sin_ref[...]
        tables = {"q": (qg_ref[...] * scale * cos, qgs_ref[...] * scale * sin),
                  "k": (kg_ref[...] * cos, kgs_ref[...] * sin)}
    else:
        q_ref, k_ref, v_ref = rest

    def head(col, kind):
        zi = z[:, col:col + HEAD_DIM]
        inv = lax.rsqrt(jnp.mean(zi * zi, axis=-1, keepdims=True) + QK_EPS)
        if rope:
            gc, gs = tables[kind]
            return inv * (zi * gc + zp[:, col:col + HEAD_DIM] * gs)
        if kind == "q":
            return zi * inv * (qg_ref[...] * scale)
        return zi * inv * kg_ref[...]

    for i in range(N_HEADS):
        q_ref[:, i * HEAD_DIM:(i + 1) * HEAD_DIM] = head(i * HEAD_DIM, "q").astype(q_ref.dtype)
    for i in range(N_KV_HEADS):
        cols = slice(i * HEAD_DIM, (i + 1) * HEAD_DIM)
        k_ref[:, cols] = head(Q_DIM + i * HEAD_DIM, "k").astype(k_ref.dtype)
        v_ref[:, cols] = z[:, QK_DIM + i * HEAD_DIM:QK_DIM + (i + 1) * HEAD_DIM].astype(v_ref.dtype)


def _qkv(x, mods, W, j, layer, rope_tables, kv_dtype, per_batch):
    t = x.shape[0]
    rope = rope_tables is not None
    row = pl.BlockSpec((ROW_TILE, D_MODEL), lambda i: (i, 0))
    kv_row = pl.BlockSpec((ROW_TILE, KV_DIM), lambda i: (i, 0))
    names = ["at_qkv_w", "at_qkv_b", "at_q_gain", "at_k_gain"]
    if rope:
        names += ["at_qk_w_swap", "at_qk_b_swap", "at_q_gain_swap", "at_k_gain_swap"]
    in_specs = [row, _mods_spec(layer, per_batch)] + [_layer_block(W[n], j) for n in names]
    args = [x, mods] + [W[n] for n in names]
    if rope:
        in_specs += [_resident((ROW_TILE, HEAD_DIM))] * 2
        args += list(rope_tables)
    return pl.pallas_call(
        functools.partial(_qkv_kernel, rope, per_batch),
        grid=(t // ROW_TILE,),
        in_specs=in_specs,
        out_specs=[row, kv_row, kv_row],
        out_shape=[jax.ShapeDtypeStruct((t, Q_DIM), BF16), jax.ShapeDtypeStruct((t, KV_DIM), kv_dtype),
                   jax.ShapeDtypeStruct((t, KV_DIM), kv_dtype)],
        compiler_params=_params(1),
        name="qkv_rope" if rope else "qkv",
    )(*args)


def _attn_kernel(has_ctx, seq_len, q_ref, k_ref, v_ref, *rest):
    if has_ctx:
        kc_ref, vc_ref, o_ref = rest
        kc = kc_ref[...]
        vc = jnp.concatenate([vc_ref[...], jnp.ones(vc_ref.shape, BF16)], axis=1)
    else:
        (o_ref,) = rest
    nt = (((1,), (1,)), ((), ()))
    n_seq = k_ref.shape[0] // seq_len
    q_rows = q_ref.shape[0] // n_seq
    ones = jnp.ones((seq_len, HEAD_DIM), BF16)
    for s in range(n_seq):
        k = k_ref[s * seq_len:(s + 1) * seq_len, :].astype(BF16)
        v = jnp.concatenate([v_ref[s * seq_len:(s + 1) * seq_len, :].astype(BF16), ones], axis=1)
        rows = slice(s * q_rows, (s + 1) * q_rows)
        for g0 in range(0, GROUP, HEAD_STACK):
            heads = range(g0, g0 + HEAD_STACK)
            q = jnp.concatenate([q_ref[rows, g * HEAD_DIM:(g + 1) * HEAD_DIM] for g in heads], axis=0)
            sc_new = lax.dot_general(q, k, nt, preferred_element_type=F32)
            mx = jnp.max(sc_new, axis=-1, keepdims=True)
            if has_ctx:
                sc_ctx = lax.dot_general(q, kc, nt, preferred_element_type=F32)
                mx = jnp.maximum(mx, jnp.max(sc_ctx, axis=-1, keepdims=True))
            o = jnp.dot(jnp.exp2(sc_new - mx).astype(BF16), v, preferred_element_type=F32)
            if has_ctx:
                o = o + jnp.dot(jnp.exp2(sc_ctx - mx).astype(BF16), vc, preferred_element_type=F32)
            o = (o[:, :HEAD_DIM] / o[:, HEAD_DIM:]).astype(BF16)
            for n, g in enumerate(heads):
                o_ref[rows, g * HEAD_DIM:(g + 1) * HEAD_DIM] = o[n * q_rows:(n + 1) * q_rows]


def _attention(q, k, v, ctx, seq_len, q_tile):
    t = q.shape[0]
    gw = GROUP * HEAD_DIM
    kv_rows = max(seq_len, q_tile)
    n_b = t // kv_rows
    n_q = kv_rows // q_tile
    q_spec = pl.BlockSpec((q_tile, gw), lambda b, h, i: (b * n_q + i, h))
    kv_spec = pl.BlockSpec((kv_rows, HEAD_DIM), lambda b, h, i: (b, h))
    in_specs = [q_spec, kv_spec, kv_spec]
    args = [q, k, v]
    if ctx is not None:
        kc, vc, layer = ctx
        ctx_spec = pl.BlockSpec((None, None, kc.shape[2], HEAD_DIM), lambda b, h, i: (b, layer, 0, h))
        in_specs += [ctx_spec, ctx_spec]
        args += [kc, vc]
    return pl.pallas_call(
        functools.partial(_attn_kernel, ctx is not None, seq_len),
        grid=(n_b, N_KV_HEADS, n_q),
        in_specs=in_specs,
        out_specs=q_spec,
        out_shape=jax.ShapeDtypeStruct((t, Q_DIM), BF16),
        compiler_params=_params(3),
        name="attention_ctx" if ctx is not None else "attention",
    )(*args)


def _hyena_feats(seq_len):
    t = np.linspace(0.0, 1.0, seq_len)[:, None]
    n_bands = (POS_EMB_DIM - 1) // 2
    w = 2.0 * math.pi * np.arange(seq_len) / seq_len
    f = np.linspace(1e-4, n_bands - 1, n_bands)
    ang = w[:, None] * f[None, :]
    z = np.concatenate([t, np.cos(ang), -np.sin(ang)], -1)
    z = np.pad(z, ((0, 0), (0, FILTER_PAD - POS_EMB_DIM)))
    max_decay = math.log(DECAY_TARGET) / FAST_DECAY_PCT
    min_decay = math.log(DECAY_TARGET) / SLOW_DECAY_PCT
    deltas = np.abs(np.linspace(min_decay, max_decay, D_MODEL))[None, :]
    return z.astype(np.float32), t.astype(np.float32), deltas.astype(np.float32)


def _dft_mats(seq_len):
    n = 2 * seq_len
    idx = np.arange(seq_len)
    ang = 2.0 * math.pi * ((idx[:, None] * idx[None, :]) % n) / n
    sign = np.where(idx % 2 == 0, 1.0, -1.0)
    fa, fb = np.cos(ang), -np.sin(ang)
    fb[0, :] = sign
    fm = np.concatenate([fa, fb], 0)
    ga, gb = (2.0 / n) * np.cos(ang), -(2.0 / n) * np.sin(ang)
    ga[:, 0] = 1.0 / n
    gb[:, 0] = sign / n
    time_of = (np.arange(seq_len) % (seq_len // PHASES)) * PHASES + np.arange(seq_len) // (seq_len // PHASES)
    return (fm.astype(np.float32), fm[:, time_of].astype(np.float32),
            ga[time_of, :].astype(np.float32), gb[time_of, :].astype(np.float32))


def _rope_tables(seq_len):
    rows = np.repeat(np.arange(seq_len // GRID_W), GRID_W).astype(np.float64)
    cols = np.tile(np.arange(GRID_W), seq_len // GRID_W).astype(np.float64)
    half = HEAD_DIM // 2
    inv = ROPE_THETA ** (-np.arange(0, half, 2, dtype=np.float64) / half)
    ang = np.concatenate([rows[:, None] * inv, cols[:, None] * inv], -1)
    cos = np.repeat(np.cos(ang), 2, axis=-1)
    sin = np.stack([-np.sin(ang), np.sin(ang)], axis=-1).reshape(seq_len, HEAD_DIM)
    return cos.astype(np.float32), sin.astype(np.float32)


def _pair_swap_matrix(n):
    p = np.zeros((n, n), np.float32)
    idx = np.arange(n)
    p[idx ^ 1, idx] = 1.0
    return p


def _trunk(x, mods, seq_len, per_batch, W, rope, ctx):
    fm, fm_pm, ga_pm, gb_pm = (jnp.asarray(a).astype(BF16) for a in _dft_mats(seq_len))
    feats, t_col, deltas = (jnp.asarray(a) for a in _hyena_feats(seq_len))
    new_k, new_v = [], []
    for l in range(DEPTH):
        j = l // 2
        if l % 2 == 0:
            ha, hb, hc = _hyena_filter((feats, t_col, deltas, fm), W, j, seq_len)
            x0, u = _hyena_in(x, mods, W, j, l, seq_len, per_batch)
            a = _dft_conv(u, x0, ha, hb, hc, fm_pm, ga_pm, gb_pm, seq_len)
            w_o, b_o = W["hy_out_w"], W["hy_out_b"]
        else:
            if ctx is None:
                q, k, v = _qkv(x, mods, W, j, l, None, F32, per_batch)
                new_k.append(k)
                new_v.append(v)
                a = _attention(q, k, v, None, seq_len, ROW_TILE)
            else:
                q, k, v = _qkv(x, mods, W, j, l, rope, BF16, per_batch)
                a = _attention(q, k, v, (ctx[0], ctx[1], j), seq_len, CTX_Q_TILE)
            w_o, b_o = W["at_o_w"], W["at_o_b"]
        x = _proj_ln(a, x, mods, w_o, b_o, j, W, l, per_batch, l % 2 == 0)
        x = _ffn(x, mods, W, l, seq_len, per_batch)
    return x, new_k, new_v


def kernel(x_prompt, x_sample, cache_k, cache_v, c, c_ctx, w_mod, b_mod, ln_g, ln_b, hy_in_w, hy_in_b, hy_short_w, hy_short_b, hy_pos_w1, hy_pos_b1, hy_pos_w2, hy_pos_b2, hy_pos_wout, hy_freq, hy_filt_bias, hy_out_w, hy_out_b, at_qkv_w, at_qkv_b, at_q_gain, at_k_gain, at_o_w, at_o_b, ff_in_w, ff_in_b, ff_conv_w, ff_conv_b, ff_out_w, ff_out_b):
    batch, seq, _ = x_prompt.shape
    dec_batch, dec_seq, _ = x_sample.shape
    assert dec_seq == ROW_TILE and ROW_TILE % seq == 0 and 1 + dec_batch <= COND_ROWS
    n_attn = at_qkv_w.shape[0]
    pad = FILTER_PAD - FILTER_WIDTH

    def vec(a):
        return a[:, None, :]

    qkv_w = at_qkv_w.astype(BF16)
    swap = jnp.asarray(_pair_swap_matrix(QK_DIM))
    swap_head = jnp.asarray(_pair_swap_matrix(HEAD_DIM))
    hi = lax.Precision.HIGHEST
    W = {
        "ln_g": ln_g[:, :, None, :], "ln_b": ln_b[:, :, None, :],
        "hy_in_w": hy_in_w.astype(BF16), "hy_in_b": vec(hy_in_b),
        "hy_short_w": hy_short_w, "hy_short_b": vec(hy_short_b),
        "hy_pos_w1": jnp.pad(hy_pos_w1, ((0, 0), (0, FILTER_PAD - POS_EMB_DIM), (0, pad))),
        "hy_pos_b1": vec(jnp.pad(hy_pos_b1, ((0, 0), (0, pad)))),
        "hy_pos_w2": jnp.pad(hy_pos_w2, ((0, 0), (0, 0), (0, pad), (0, pad))),
        "hy_pos_b2": jnp.pad(hy_pos_b2, ((0, 0), (0, 0), (0, pad)))[:, :, None, :],
        "hy_freq": vec(jnp.pad(hy_freq, ((0, 0), (0, pad)))),
        "hy_pos_wout": hy_pos_wout, "hy_filt_bias": vec(hy_filt_bias),
        "hy_out_w": hy_out_w.astype(BF16), "hy_out_b": vec(hy_out_b),
        "at_qkv_w": qkv_w, "at_qkv_b": vec(at_qkv_b),
        "at_q_gain": vec(at_q_gain), "at_k_gain": vec(at_k_gain),
        "at_qk_w_swap": jnp.dot(qkv_w[:, :, :QK_DIM], swap.astype(BF16), preferred_element_type=F32).astype(BF16),
        "at_qk_b_swap": vec(jnp.dot(at_qkv_b[:, :QK_DIM], swap, precision=hi)),
        "at_q_gain_swap": vec(jnp.dot(at_q_gain, swap_head, precision=hi)),
        "at_k_gain_swap": vec(jnp.dot(at_k_gain, swap_head, precision=hi)),
        "at_o_w": at_o_w.astype(BF16), "at_o_b": vec(at_o_b),
        "ff_in_w": ff_in_w.astype(BF16), "ff_in_b": vec(ff_in_b),
        "ff_conv_w": ff_conv_w, "ff_conv_b": vec(ff_conv_b),
        "ff_out_w": ff_out_w.astype(BF16), "ff_out_b": vec(ff_out_b),
    }
    rope = tuple(jnp.asarray(a) for a in _rope_tables(dec_seq))

    cond = jnp.concatenate([c_ctx[None, :], c, jnp.zeros((COND_ROWS - 1 - dec_batch, D_MODEL), F32)], 0)
    mods = _modulation_all(cond, w_mod, b_mod)

    past = cache_k.shape[2]
    ctx_k = cache_k.astype(BF16).reshape(dec_batch, n_attn, past, KV_DIM)
    ctx_v = cache_v.astype(BF16).reshape(dec_batch, n_attn, past, KV_DIM)

    y_p, new_k, new_v = _trunk(x_prompt.reshape(batch * seq, D_MODEL), mods, seq, False, W, None, None)
    y_s, _, _ = _trunk(x_sample.reshape(dec_batch * dec_seq, D_MODEL), mods, dec_seq, True, W, rope, (ctx_k, ctx_v))

    kv_shape = (batch, seq, N_KV_HEADS, HEAD_DIM)
    new_cache_k = jnp.stack([k.reshape(kv_shape) for k in new_k], axis=1)
    new_cache_v = jnp.stack([v.reshape(kv_shape) for v in new_v], axis=1)
    return (y_p.reshape(batch, seq, D_MODEL), y_s.reshape(dec_batch, dec_seq, D_MODEL), new_cache_k, new_cache_v)
```

```python
import functools
import math

import jax
import jax.numpy as jnp
import numpy as np
from jax import lax
from jax.experimental import pallas as pl
from jax.experimental.pallas import tpu as pltpu

D_MODEL = 1024
DEPTH = 4
GRID_W = 64
N_HEADS = 8
N_KV_HEADS = 2
HEAD_DIM = 128
GROUP = N_HEADS // N_KV_HEADS
Q_DIM = N_HEADS * HEAD_DIM
KV_DIM = N_KV_HEADS * HEAD_DIM
QK_DIM = Q_DIM + KV_DIM
QKV_DIM = Q_DIM + 2 * KV_DIM
ROPE_THETA = 10000.0
QK_EPS = 1e-6
POS_EMB_DIM = 33
FILTER_WIDTH = 64
N_INNER_MLPS = 2
FAST_DECAY_PCT = 0.3
SLOW_DECAY_PCT = 1.5
DECAY_TARGET = 1e-2
MOD_SHIFT = 0.0
D_FF = 2816
LN_EPS = 1e-5
N_MOD = 6
DN_ALPHA = (2 * DEPTH) ** 0.25

F32 = jnp.float32
BF16 = jnp.bfloat16

LANES = 128
SUBLANES = 8
MXU_WIDTH = 256
ROW_TILE = 1024
FF_CHUNK = MXU_WIDTH
HY_CHUNK = MXU_WIDTH
PHASES = 4
PHASE_ROWS = ROW_TILE // PHASES
GELU_K = math.sqrt(2.0 / math.pi)
LOG2_E = math.log2(math.e)
HEAD_STACK = 1
CTX_Q_TILE = 512
MOD_COL_TILE = 1536
COND_ROWS = 16
FILTER_PAD = LANES
VMEM_LIMIT = 56 * 1024 * 1024


def _params(n_axes):
    return pltpu.CompilerParams(dimension_semantics=("arbitrary",) * n_axes,
                                vmem_limit_bytes=VMEM_LIMIT)


def _resident(shape):
    nd = len(shape)
    return pl.BlockSpec(shape, lambda *_: (0,) * nd, pipeline_mode=pl.Buffered(1))


def _layer_block(arr, *lead):
    shape = arr.shape[len(lead):]
    idx = tuple(lead) + (0,) * len(shape)
    return pl.BlockSpec((None,) * len(lead) + shape, lambda *_: idx, pipeline_mode=pl.Buffered(1))


def _mods_spec(layer, per_batch):
    if per_batch:
        return pl.BlockSpec((None, SUBLANES, N_MOD * D_MODEL), lambda i: (layer, (i + 1) // SUBLANES, 0))
    return pl.BlockSpec((None, SUBLANES, N_MOD * D_MODEL), lambda i: (layer, 0, 0))


def _mod_row(mod_ref, per_batch):
    row = (pl.program_id(0) + 1) % SUBLANES if per_batch else 0
    return mod_ref[pl.ds(row, 1), :]


def _mod(m, idx):
    return m[:, idx * D_MODEL:(idx + 1) * D_MODEL]


def _modulate(x, m, shift_idx):
    return x * (1.0 + _mod(m, shift_idx + 1)) + _mod(m, shift_idx)


def _residual_layer_norm(x, out, gate, g, b):
    y = DN_ALPHA * x + (1.0 + gate) * out
    mu = jnp.mean(y, axis=-1, keepdims=True)
    yc = y - mu
    var = jnp.mean(yc * yc, axis=-1, keepdims=True)
    return yc * lax.rsqrt(var + LN_EPS) * g + b


def _column_slab_specs():
    return [pl.BlockSpec((ROW_TILE, LANES), functools.partial(lambda j, i: (i, j), j)) for j in range(D_MODEL // LANES)]


def _read_phase(x_refs, p):
    return jnp.concatenate([r[pl.ds(p, PHASE_ROWS, stride=PHASES), :] for r in x_refs], axis=1)


def _slab_drain(slab_ref, o_ref):
    for j in range(D_MODEL // LANES):
        o_ref[:, j * LANES:(j + 1) * LANES] = slab_ref[j]


def _slab_write_phase(slab_ref, p, y):
    for j in range(D_MODEL // LANES):
        slab_ref[j, pl.ds(p, PHASE_ROWS, stride=PHASES), :] = y[:, j * LANES:(j + 1) * LANES]


def _slab_scratch():
    return pltpu.VMEM((D_MODEL // LANES, ROW_TILE, LANES), F32)


def _phase_bounds(seq_len):
    per_seq = seq_len // PHASES
    pos = lax.broadcasted_iota(jnp.int32, (PHASE_ROWS, 1), 0) & (per_seq - 1)
    return pos == 0, pos == per_seq - 1


def _phase_conv3(z, w, bias, conv_bias, seq_start, seq_end, scale=1.0):
    w = w * scale
    b_all = bias * (w[0:1] + w[1:2] + w[2:3]) + conv_bias * scale
    blk = [z[p * PHASE_ROWS:(p + 1) * PHASE_ROWS] for p in range(PHASES)]
    prev0 = jnp.where(seq_start, -bias, pltpu.roll(blk[PHASES - 1], 1, 0))
    next_last = jnp.where(seq_end, -bias, pltpu.roll(blk[0], PHASE_ROWS - 1, 0))
    out = []
    for p in range(PHASES):
        zp = prev0 if p == 0 else blk[p - 1]
        zn = next_last if p == PHASES - 1 else blk[p + 1]
        out.append(zp * w[0:1] + blk[p] * w[1:2] + zn * w[2:3] + b_all)
    return out


def _mod_kernel(cond_ref, w_ref, b_ref, o_ref):
    c = cond_ref[...]
    s = (c * jax.nn.sigmoid(c)).astype(BF16)
    o_ref[0] = jnp.dot(s, w_ref[0].astype(BF16), preferred_element_type=F32) + b_ref[0]


def _modulation_all(cond, w_mod, b_mod):
    n_col = (N_MOD * D_MODEL) // MOD_COL_TILE
    return pl.pallas_call(
        _mod_kernel,
        grid=(DEPTH, n_col),
        in_specs=[pl.BlockSpec((COND_ROWS, D_MODEL), lambda l, j: (0, 0)),
                  pl.BlockSpec((1, D_MODEL, MOD_COL_TILE), lambda l, j: (l, 0, j)),
                  pl.BlockSpec((1, 1, MOD_COL_TILE), lambda l, j: (l, 0, j))],
        out_specs=pl.BlockSpec((1, COND_ROWS, MOD_COL_TILE), lambda l, j: (l, 0, j)),
        out_shape=jax.ShapeDtypeStruct((DEPTH, COND_ROWS, N_MOD * D_MODEL), F32),
        compiler_params=_params(2),
        name="modulation",
    )(cond, w_mod, b_mod.reshape(DEPTH, 1, N_MOD * D_MODEL))


def _hyena_in_kernel(seq_len, per_batch, *refs):
    n_slab = D_MODEL // LANES
    x_refs, (mod_ref, w_ref, b_ref, sw_ref, sb_ref, x0_ref, u_ref, h_ref) = refs[:n_slab], refs[n_slab:]
    m = _mod_row(mod_ref, per_batch)
    for p in range(PHASES):
        h_ref[p * PHASE_ROWS:(p + 1) * PHASE_ROWS, :] = _modulate(_read_phase(x_refs, p), m, 0).astype(BF16)
    seq_start, seq_end = _phase_bounds(seq_len)
    for c in range(D_MODEL // HY_CHUNK):
        conv = []
        for part in range(3):
            cs = slice(part * D_MODEL + c * HY_CHUNK, part * D_MODEL + (c + 1) * HY_CHUNK)
            z = jnp.dot(h_ref[...], w_ref[:, cs], preferred_element_type=F32)
            conv.append(_phase_conv3(z, sw_ref[:, cs], b_ref[:, cs], sb_ref[:, cs], seq_start, seq_end))
        cs = slice(c * HY_CHUNK, (c + 1) * HY_CHUNK)
        for p, (x0, x1, v) in enumerate(zip(*conv)):
            rs = slice(p * PHASE_ROWS, (p + 1) * PHASE_ROWS)
            x0_ref[rs, cs] = x0.astype(BF16)
            u_ref[rs, cs] = (v * x1).astype(BF16)


def _hyena_in(x, mods, W, j, layer, seq_len, per_batch):
    t = x.shape[0]
    row = pl.BlockSpec((ROW_TILE, D_MODEL), lambda i: (i, 0))
    return pl.pallas_call(
        functools.partial(_hyena_in_kernel, seq_len, per_batch),
        grid=(t // ROW_TILE,),
        in_specs=_column_slab_specs() + [
            _mods_spec(layer, per_batch),
            _layer_block(W["hy_in_w"], j), _layer_block(W["hy_in_b"], j),
            _layer_block(W["hy_short_w"], j), _layer_block(W["hy_short_b"], j)],
        out_specs=[row, row],
        out_shape=[jax.ShapeDtypeStruct((t, D_MODEL), BF16)] * 2,
        scratch_shapes=[pltpu.VMEM((ROW_TILE, D_MODEL), BF16)],
        compiler_params=_params(1),
        name="hyena_in",
    )(*([x] * (D_MODEL // LANES)), mods, W["hy_in_w"], W["hy_in_b"], W["hy_short_w"], W["hy_short_b"])


def _hyena_filter_kernel(seq_len, z_ref, w1_ref, b1_ref, w2_ref, b2_ref, freq_ref, wo0_ref, wo1_ref,
                         t_ref, delta_ref, bias_ref, fm_ref, ha_ref, hb_ref, hc_ref, h_ref):
    hi = lax.Precision.HIGHEST

    @pl.when(pl.program_id(0) == 0)
    def _():
        freq = freq_ref[...]
        h = jnp.sin(freq * (jnp.dot(z_ref[...], w1_ref[...], precision=hi, preferred_element_type=F32)
                            + b1_ref[...]))
        for i in range(N_INNER_MLPS):
            h = jnp.sin(freq * (jnp.dot(h, w2_ref[i], precision=hi, preferred_element_type=F32) + b2_ref[i]))
        h_ref[...] = h

    h = h_ref[...]
    decay = jnp.exp(-t_ref[...] * delta_ref[...]) + MOD_SHIFT
    zero_rows = jnp.zeros((FILTER_PAD - FILTER_WIDTH, wo0_ref.shape[1]), F32)

    def taps(wo_ref):
        wo = jnp.concatenate([wo_ref[...], zero_rows], axis=0)
        return jnp.dot(h, wo, precision=hi, preferred_element_type=F32) * decay

    k0, k1 = taps(wo0_ref), taps(wo1_ref)
    row0 = lax.broadcasted_iota(jnp.int32, (seq_len, 1), 0) == 0
    k0 = jnp.where(row0, k0 + bias_ref[...], k0)
    k1 = jnp.where(row0, 0.0, k1)
    fm = fm_ref[...]
    s0 = jnp.dot(fm, k0.astype(BF16), preferred_element_type=F32)
    s1 = jnp.dot(fm, k1.astype(BF16), preferred_element_type=F32)
    ha = s0[:seq_len] + s1[:seq_len]
    s0b, s1b = s0[seq_len:], s1[seq_len:]
    nyq = s0b + s1b
    ha_ref[...] = ha
    hb_ref[...] = jnp.where(row0, 0.0, s0b - s1b)
    hc_ref[...] = jnp.where(row0, nyq, ha)


def _hyena_filter(consts, W, j, seq_len):
    feats, t_col, deltas, fm = consts
    dt = MXU_WIDTH
    n_d = D_MODEL // dt
    out = pl.BlockSpec((seq_len, dt), lambda i: (0, i))
    wout = W["hy_pos_wout"]
    return pl.pallas_call(
        functools.partial(_hyena_filter_kernel, seq_len),
        grid=(n_d,),
        in_specs=[_resident((seq_len, FILTER_PAD)),
                  _layer_block(W["hy_pos_w1"], j), _layer_block(W["hy_pos_b1"], j),
                  _layer_block(W["hy_pos_w2"], j), _layer_block(W["hy_pos_b2"], j),
                  _layer_block(W["hy_freq"], j),
                  pl.BlockSpec((None, FILTER_WIDTH, dt), lambda i: (j, 0, i)),
                  pl.BlockSpec((None, FILTER_WIDTH, dt), lambda i: (j, 0, n_d + i)),
                  _resident((seq_len, 1)), pl.BlockSpec((1, dt), lambda i: (0, i)),
                  pl.BlockSpec((None, 1, dt), lambda i: (j, 0, i)),
                  _resident((2 * seq_len, seq_len))],
        out_specs=[out, out, out],
        out_shape=[jax.ShapeDtypeStruct((seq_len, D_MODEL), F32)] * 3,
        scratch_shapes=[pltpu.VMEM((seq_len, FILTER_PAD), F32)],
        compiler_params=_params(1),
        name="hyena_filter",
    )(feats, W["hy_pos_w1"], W["hy_pos_b1"], W["hy_pos_w2"], W["hy_pos_b2"], W["hy_freq"], wout, wout,
      t_col, deltas, W["hy_filt_bias"], fm)


def _dft_conv_kernel(seq_len, u_ref, x0_ref, ha_ref, hb_ref, hc_ref, fm_ref, ga_ref, gb_ref, o_ref):
    ha, hb, hc = ha_ref[...], hb_ref[...], hc_ref[...]
    per = seq_len // PHASES
    for s in range(ROW_TILE // seq_len):
        rows = [slice(p * PHASE_ROWS + s * per, p * PHASE_ROWS + (s + 1) * per) for p in range(PHASES)]
        u = jnp.concatenate([u_ref[r, :] for r in rows], axis=0)
        spec = jnp.dot(fm_ref[...], u, preferred_element_type=F32)
        a, b = spec[:seq_len], spec[seq_len:]
        ya = (a * ha - b * hb).astype(BF16)
        yb = (a * hb + b * hc).astype(BF16)
        y = (jnp.dot(ga_ref[...], ya, preferred_element_type=F32)
             + jnp.dot(gb_ref[...], yb, preferred_element_type=F32))
        for p, r in enumerate(rows):
            o_ref[r, :] = (x0_ref[r, :].astype(F32) * y[p * per:(p + 1) * per]).astype(BF16)


def _dft_conv(u, x0, ha, hb, hc, fm, ga, gb, seq_len):
    t = u.shape[0]
    dt = 512 if seq_len == ROW_TILE else D_MODEL
    row = pl.BlockSpec((ROW_TILE, dt), lambda j, i: (i, j))
    filt = pl.BlockSpec((seq_len, dt), lambda j, i: (0, j))
    return pl.pallas_call(
        functools.partial(_dft_conv_kernel, seq_len),
        grid=(D_MODEL // dt, t // ROW_TILE),
        in_specs=[row, row, filt, filt, filt,
                  _resident((2 * seq_len, seq_len)), _resident((seq_len, seq_len)),
                  _resident((seq_len, seq_len))],
        out_specs=row,
        out_shape=jax.ShapeDtypeStruct((t, D_MODEL), BF16),
        compiler_params=_params(2),
        name="hyena_dft_conv",
    )(u, x0, ha, hb, hc, fm, ga, gb)


def _proj_ln_kernel(per_batch, phase_major, a_ref, x_ref, mod_ref, w_ref, b_ref, g_ref, beta_ref, o_ref, *scratch):
    out = jnp.dot(a_ref[...], w_ref[...], preferred_element_type=F32) + b_ref[...]
    if phase_major:
        (slab_ref,) = scratch
        for p in range(PHASES):
            _slab_write_phase(slab_ref, p, out[p * PHASE_ROWS:(p + 1) * PHASE_ROWS])
        out = jnp.concatenate([slab_ref[j] for j in range(D_MODEL // LANES)], axis=1)
    gate = _mod(_mod_row(mod_ref, per_batch), 2)
    o_ref[...] = _residual_layer_norm(x_ref[...], out, gate, g_ref[...], beta_ref[...])


def _proj_ln(a, x, mods, w, b, j, W, layer, per_batch, phase_major):
    t = x.shape[0]
    row = pl.BlockSpec((ROW_TILE, D_MODEL), lambda i: (i, 0))
    return pl.pallas_call(
        functools.partial(_proj_ln_kernel, per_batch, phase_major),
        grid=(t // ROW_TILE,),
        in_specs=[row, row, _mods_spec(layer, per_batch), _layer_block(w, j), _layer_block(b, j),
                  _layer_block(W["ln_g"], layer, 0), _layer_block(W["ln_b"], layer, 0)],
        out_specs=row,
        out_shape=jax.ShapeDtypeStruct((t, D_MODEL), F32),
        scratch_shapes=[_slab_scratch()] if phase_major else [],
        compiler_params=_params(1),
        name="proj_ln_phase" if phase_major else "proj_ln",
    )(a, x, mods, w, b, W["ln_g"], W["ln_b"])


def _ffn_kernel(seq_len, per_batch, *refs):
    n_slab = D_MODEL // LANES
    x_refs = refs[:n_slab]
    (mod_ref, wi_ref, bi_ref, cw_ref, cb_ref, wo_ref, bo_ref, g_ref, beta_ref,
     o_ref, acc_ref, h_ref, z_ref, slab_ref) = refs[n_slab:]
    m = _mod_row(mod_ref, per_batch)
    for p in range(PHASES):
        h_ref[p * PHASE_ROWS:(p + 1) * PHASE_ROWS, :] = _modulate(_read_phase(x_refs, p), m, 3).astype(BF16)
    seq_start, seq_end = _phase_bounds(seq_len)
    n_c = D_FF // FF_CHUNK

    def cols(c, part):
        return pl.ds(pl.multiple_of(part * D_FF + c * FF_CHUNK, FF_CHUNK), FF_CHUNK)

    def first_matmul(c, slot):
        for part in range(2):
            z_ref[slot, part] = jnp.dot(h_ref[...], wi_ref[:, cols(c, part)], preferred_element_type=F32)

    def gate_and_second_matmul(c, slot, init=False):
        conv = [_phase_conv3(z_ref[slot, part], cw_ref[:, cols(c, part)], bi_ref[:, cols(c, part)],
                             cb_ref[:, cols(c, part)], seq_start, seq_end, scale=(1.0, 0.5)[part])
                for part in range(2)]
        a = []
        for g, half_v in zip(*conv):
            t = jnp.tanh(g * (GELU_K + (GELU_K * 0.044715) * (g * g)))
            gv = g * half_v
            a.append((gv + gv * t).astype(BF16))
        a = jnp.concatenate(a, axis=0)
        row = pl.multiple_of(c * FF_CHUNK, FF_CHUNK)
        upd = jnp.dot(a, wo_ref[pl.ds(row, FF_CHUNK), :], preferred_element_type=F32)
        if init:
            acc_ref[...] = upd
        else:
            acc_ref[...] += upd

    first_matmul(0, 0)
    first_matmul(1, 1)
    gate_and_second_matmul(0, 0, init=True)

    def pair(i, carry):
        c = 2 * i + 1
        first_matmul(c + 1, 0)
        gate_and_second_matmul(c, 1)
        first_matmul(c + 2, 1)
        gate_and_second_matmul(c + 1, 0)
        return carry

    lax.fori_loop(0, (n_c - 3) // 2, pair, 0)
    first_matmul(n_c - 1, 0)
    gate_and_second_matmul(n_c - 2, 1)
    gate_and_second_matmul(n_c - 1, 0)
    for p in range(PHASES):
        out = acc_ref[p * PHASE_ROWS:(p + 1) * PHASE_ROWS, :] + bo_ref[...]
        y = _residual_layer_norm(_read_phase(x_refs, p), out, _mod(m, 5), g_ref[...], beta_ref[...])
        _slab_write_phase(slab_ref, p, y)
    _slab_drain(slab_ref, o_ref)


def _ffn(x, mods, W, layer, seq_len, per_batch):
    t = x.shape[0]
    assert (D_FF // FF_CHUNK) % 2 == 1
    row = pl.BlockSpec((ROW_TILE, D_MODEL), lambda i: (i, 0))
    names = ["ff_in_w", "ff_in_b", "ff_conv_w", "ff_conv_b", "ff_out_w", "ff_out_b"]
    return pl.pallas_call(
        functools.partial(_ffn_kernel, seq_len, per_batch),
        grid=(t // ROW_TILE,),
        in_specs=_column_slab_specs() + [_mods_spec(layer, per_batch)] + [_layer_block(W[n], layer) for n in names]
                 + [_layer_block(W["ln_g"], layer, 1), _layer_block(W["ln_b"], layer, 1)],
        out_specs=row,
        out_shape=jax.ShapeDtypeStruct((t, D_MODEL), F32),
        scratch_shapes=[pltpu.VMEM((ROW_TILE, D_MODEL), F32), pltpu.VMEM((ROW_TILE, D_MODEL), BF16),
                        pltpu.VMEM((2, 2, ROW_TILE, FF_CHUNK), F32), _slab_scratch()],
        compiler_params=_params(1),
        name="conv_ffn",
    )(*([x] * (D_MODEL // LANES)), mods, *[W[n] for n in names], W["ln_g"], W["ln_b"])


def _qkv_kernel(rope, per_batch, x_ref, mod_ref, w_ref, b_ref, qg_ref, kg_ref, *rest):
    scale = LOG2_E * HEAD_DIM ** -0.5
    h = _modulate(x_ref[...], _mod_row(mod_ref, per_batch), 0).astype(BF16)
    z = jnp.dot(h, w_ref[...], preferred_element_type=F32) + b_ref[...]
    if rope:
        ws_ref, bs_ref, qgs_ref, kgs_ref, cos_ref, sin_ref, q_ref, k_ref, v_ref = rest
        zp = jnp.dot(h, ws_ref[...], preferred_element_type=F32) + bs_ref[...]
        cos, sin = cos_ref[...], sin_ref[...]
        tables = {"q": (qg_ref[...] * scale * cos, qgs_ref[...] * scale * sin),
                  "k": (kg_ref[...] * cos, kgs_ref[...] * sin)}
    else:
        q_ref, k_ref, v_ref = rest

    def head(col, kind):
        zi = z[:, col:col + HEAD_DIM]
        inv = lax.rsqrt(jnp.mean(zi * zi, axis=-1, keepdims=True) + QK_EPS)
        if rope:
            gc, gs = tables[kind]
            return inv * (zi * gc + zp[:, col:col + HEAD_DIM] * gs)
        if kind == "q":
            return zi * inv * (qg_ref[...] * scale)
        return zi * inv * kg_ref[...]

    for i in range(N_HEADS):
        q_ref[:, i * HEAD_DIM:(i + 1) * HEAD_DIM] = head(i * HEAD_DIM, "q").astype(q_ref.dtype)
    for i in range(N_KV_HEADS):
        cols = slice(i * HEAD_DIM, (i + 1) * HEAD_DIM)
        k_head = head(Q_DIM + i * HEAD_DIM, "k")
        v_head = z[:, QK_DIM + i * HEAD_DIM:QK_DIM + (i + 1) * HEAD_DIM]
        if k_ref.shape[1] == HEAD_DIM:
            rows = pl.ds(i, ROW_TILE, stride=N_KV_HEADS)
            k_ref[rows, :] = k_head.astype(k_ref.dtype)
            v_ref[rows, :] = v_head.astype(v_ref.dtype)
        else:
            k_ref[:, cols] = k_head.astype(k_ref.dtype)
            v_ref[:, cols] = v_head.astype(v_ref.dtype)


def _qkv(x, mods, W, j, layer, rope_tables, kv_dtype, per_batch):
    t = x.shape[0]
    rope = rope_tables is not None
    row = pl.BlockSpec((ROW_TILE, D_MODEL), lambda i: (i, 0))
    if kv_dtype == F32:
        kv_shape = (t * N_KV_HEADS, HEAD_DIM)
        kv_row = pl.BlockSpec((ROW_TILE * N_KV_HEADS, HEAD_DIM), lambda i: (i, 0))
    else:
        kv_shape = (t, KV_DIM)
        kv_row = pl.BlockSpec((ROW_TILE, KV_DIM), lambda i: (i, 0))
    names = ["at_qkv_w", "at_qkv_b", "at_q_gain", "at_k_gain"]
    if rope:
        names += ["at_qk_w_swap", "at_qk_b_swap", "at_q_gain_swap", "at_k_gain_swap"]
    in_specs = [row, _mods_spec(layer, per_batch)] + [_layer_block(W[n], j) for n in names]
    args = [x, mods] + [W[n] for n in names]
    if rope:
        in_specs += [_resident((ROW_TILE, HEAD_DIM))] * 2
        args += list(rope_tables)
    return pl.pallas_call(
        functools.partial(_qkv_kernel, rope, per_batch),
        grid=(t // ROW_TILE,),
        in_specs=in_specs,
        out_specs=[row, kv_row, kv_row],
        out_shape=[jax.ShapeDtypeStruct((t, Q_DIM), BF16), jax.ShapeDtypeStruct(kv_shape, kv_dtype),
                   jax.ShapeDtypeStruct(kv_shape, kv_dtype)],
        compiler_params=_params(1),
        name="qkv_rope" if rope else "qkv",
    )(*args)


def _attn_kernel(has_ctx, seq_len, q_ref, k_ref, v_ref, *rest):
    if has_ctx:
        kc_ref, vc_ref, o_ref = rest
        kc = kc_ref[...]
        vc = jnp.concatenate([vc_ref[...], jnp.ones(vc_ref.shape, BF16)], axis=1)
    else:
        (o_ref,) = rest
    nt = (((1,), (1,)), ((), ()))
    heads_in_block = N_KV_HEADS if k_ref.dtype == F32 else 1
    n_seq = k_ref.shape[0] // (seq_len * heads_in_block)
    q_rows = q_ref.shape[0] // n_seq
    ones = jnp.ones((seq_len, HEAD_DIM), BF16)
    for s in range(n_seq):
        if heads_in_block > 1:
            kv_rows = pl.ds(s * seq_len * heads_in_block + pl.program_id(1), seq_len, stride=heads_in_block)
        else:
            kv_rows = slice(s * seq_len, (s + 1) * seq_len)
        k = k_ref[kv_rows, :].astype(BF16)
        v = jnp.concatenate([v_ref[kv_rows, :].astype(BF16), ones], axis=1)
        rows = slice(s * q_rows, (s + 1) * q_rows)
        for g0 in range(0, GROUP, HEAD_STACK):
            heads = range(g0, g0 + HEAD_STACK)
            q = jnp.concatenate([q_ref[rows, g * HEAD_DIM:(g + 1) * HEAD_DIM] for g in heads], axis=0)
            sc_new = lax.dot_general(q, k, nt, preferred_element_type=F32)
            mx = jnp.max(sc_new, axis=-1, keepdims=True)
            if has_ctx:
                sc_ctx = lax.dot_general(q, kc, nt, preferred_element_type=F32)
                mx = jnp.maximum(mx, jnp.max(sc_ctx, axis=-1, keepdims=True))
            o = jnp.dot(jnp.exp2(sc_new - mx).astype(BF16), v, preferred_element_type=F32)
            if has_ctx:
                o = o + jnp.dot(jnp.exp2(sc_ctx - mx).astype(BF16), vc, preferred_element_type=F32)
            o = (o[:, :HEAD_DIM] / o[:, HEAD_DIM:]).astype(BF16)
            for n, g in enumerate(heads):
                o_ref[rows, g * HEAD_DIM:(g + 1) * HEAD_DIM] = o[n * q_rows:(n + 1) * q_rows]


def _attention(q, k, v, ctx, seq_len, q_tile):
    t = q.shape[0]
    gw = GROUP * HEAD_DIM
    kv_rows = max(seq_len, q_tile)
    n_b = t // kv_rows
    n_q = kv_rows // q_tile
    q_spec = pl.BlockSpec((q_tile, gw), lambda b, h, i: (b * n_q + i, h))
    if k.dtype == F32:
        kv_spec = pl.BlockSpec((kv_rows * N_KV_HEADS, HEAD_DIM), lambda b, h, i: (b, 0))
    else:
        kv_spec = pl.BlockSpec((kv_rows, HEAD_DIM), lambda b, h, i: (b, h))
    in_specs = [q_spec, kv_spec, kv_spec]
    args = [q, k, v]
    if ctx is not None:
        kc, vc, layer = ctx
        ctx_spec = pl.BlockSpec((None, None, kc.shape[2], HEAD_DIM), lambda b, h, i: (b, layer, 0, h))
        in_specs += [ctx_spec, ctx_spec]
        args += [kc, vc]
    return pl.pallas_call(
        functools.partial(_attn_kernel, ctx is not None, seq_len),
        grid=(n_b, N_KV_HEADS, n_q),
        in_specs=in_specs,
        out_specs=q_spec,
        out_shape=jax.ShapeDtypeStruct((t, Q_DIM), BF16),
        compiler_params=_params(3),
        name="attention_ctx" if ctx is not None else "attention",
    )(*args)


def _hyena_feats(seq_len):
    t = np.linspace(0.0, 1.0, seq_len)[:, None]
    n_bands = (POS_EMB_DIM - 1) // 2
    w = 2.0 * math.pi * np.arange(seq_len) / seq_len
    f = np.linspace(1e-4, n_bands - 1, n_bands)
    ang = w[:, None] * f[None, :]
    z = np.concatenate([t, np.cos(ang), -np.sin(ang)], -1)
    z = np.pad(z, ((0, 0), (0, FILTER_PAD - POS_EMB_DIM)))
    max_decay = math.log(DECAY_TARGET) / FAST_DECAY_PCT
    min_decay = math.log(DECAY_TARGET) / SLOW_DECAY_PCT
    deltas = np.abs(np.linspace(min_decay, max_decay, D_MODEL))[None, :]
    return z.astype(np.float32), t.astype(np.float32), deltas.astype(np.float32)


def _dft_mats(seq_len):
    n = 2 * seq_len
    idx = np.arange(seq_len)
    ang = 2.0 * math.pi * ((idx[:, None] * idx[None, :]) % n) / n
    sign = np.where(idx % 2 == 0, 1.0, -1.0)
    fa, fb = np.cos(ang), -np.sin(ang)
    fb[0, :] = sign
    fm = np.concatenate([fa, fb], 0)
    ga, gb = (2.0 / n) * np.cos(ang), -(2.0 / n) * np.sin(ang)
    ga[:, 0] = 1.0 / n
    gb[:, 0] = sign / n
    time_of = (np.arange(seq_len) % (seq_len // PHASES)) * PHASES + np.arange(seq_len) // (seq_len // PHASES)
    return (fm.astype(np.float32), fm[:, time_of].astype(np.float32),
            ga[time_of, :].astype(np.float32), gb[time_of, :].astype(np.float32))


def _rope_tables(seq_len):
    rows = np.repeat(np.arange(seq_len // GRID_W), GRID_W).astype(np.float64)
    cols = np.tile(np.arange(GRID_W), seq_len // GRID_W).astype(np.float64)
    half = HEAD_DIM // 2
    inv = ROPE_THETA ** (-np.arange(0, half, 2, dtype=np.float64) / half)
    ang = np.concatenate([rows[:, None] * inv, cols[:, None] * inv], -1)
    cos = np.repeat(np.cos(ang), 2, axis=-1)
    sin = np.stack([-np.sin(ang), np.sin(ang)], axis=-1).reshape(seq_len, HEAD_DIM)
    return cos.astype(np.float32), sin.astype(np.float32)


def _pair_swap_matrix(n):
    p = np.zeros((n, n), np.float32)
    idx = np.arange(n)
    p[idx ^ 1, idx] = 1.0
    return p


def _trunk(x, mods, seq_len, per_batch, W, rope, ctx):
    fm, fm_pm, ga_pm, gb_pm = (jnp.asarray(a).astype(BF16) for a in _dft_mats(seq_len))
    feats, t_col, deltas = (jnp.asarray(a) for a in _hyena_feats(seq_len))
    new_k, new_v = [], []
    for l in range(DEPTH):
        j = l // 2
        if l % 2 == 0:
            ha, hb, hc = _hyena_filter((feats, t_col, deltas, fm), W, j, seq_len)
            x0, u = _hyena_in(x, mods, W, j, l, seq_len, per_batch)
            a = _dft_conv(u, x0, ha, hb, hc, fm_pm, ga_pm, gb_pm, seq_len)
            w_o, b_o = W["hy_out_w"], W["hy_out_b"]
        else:
            if ctx is None:
                q, k, v = _qkv(x, mods, W, j, l, None, F32, per_batch)
                new_k.append(k)
                new_v.append(v)
                a = _attention(q, k, v, None, seq_len, ROW_TILE)
            else:
                q, k, v = _qkv(x, mods, W, j, l, rope, BF16, per_batch)
                a = _attention(q, k, v, (ctx[0], ctx[1], j), seq_len, CTX_Q_TILE)
            w_o, b_o = W["at_o_w"], W["at_o_b"]
        x = _proj_ln(a, x, mods, w_o, b_o, j, W, l, per_batch, l % 2 == 0)
        x = _ffn(x, mods, W, l, seq_len, per_batch)
    return x, new_k, new_v


def kernel(x_prompt, x_sample, cache_k, cache_v, c, c_ctx, w_mod, b_mod, ln_g, ln_b, hy_in_w, hy_in_b, hy_short_w, hy_short_b, hy_pos_w1, hy_pos_b1, hy_pos_w2, hy_pos_b2, hy_pos_wout, hy_freq, hy_filt_bias, hy_out_w, hy_out_b, at_qkv_w, at_qkv_b, at_q_gain, at_k_gain, at_o_w, at_o_b, ff_in_w, ff_in_b, ff_conv_w, ff_conv_b, ff_out_w, ff_out_b):
    batch, seq, _ = x_prompt.shape
    dec_batch, dec_seq, _ = x_sample.shape
    assert dec_seq == ROW_TILE and ROW_TILE % seq == 0 and 1 + dec_batch <= COND_ROWS
    n_attn = at_qkv_w.shape[0]
    pad = FILTER_PAD - FILTER_WIDTH

    def vec(a):
        return a[:, None, :]

    qkv_w = at_qkv_w.astype(BF16)
    swap = jnp.asarray(_pair_swap_matrix(QK_DIM))
    swap_head = jnp.asarray(_pair_swap_matrix(HEAD_DIM))
    hi = lax.Precision.HIGHEST
    W = {
        "ln_g": ln_g[:, :, None, :], "ln_b": ln_b[:, :, None, :],
        "hy_in_w": hy_in_w.astype(BF16), "hy_in_b": vec(hy_in_b),
        "hy_short_w": hy_short_w, "hy_short_b": vec(hy_short_b),
        "hy_pos_w1": jnp.pad(hy_pos_w1, ((0, 0), (0, FILTER_PAD - POS_EMB_DIM), (0, pad))),
        "hy_pos_b1": vec(jnp.pad(hy_pos_b1, ((0, 0), (0, pad)))),
        "hy_pos_w2": jnp.pad(hy_pos_w2, ((0, 0), (0, 0), (0, pad), (0, pad))),
        "hy_pos_b2": jnp.pad(hy_pos_b2, ((0, 0), (0, 0), (0, pad)))[:, :, None, :],
        "hy_freq": vec(jnp.pad(hy_freq, ((0, 0), (0, pad)))),
        "hy_pos_wout": hy_pos_wout, "hy_filt_bias": vec(hy_filt_bias),
        "hy_out_w": hy_out_w.astype(BF16), "hy_out_b": vec(hy_out_b),
        "at_qkv_w": qkv_w, "at_qkv_b": vec(at_qkv_b),
        "at_q_gain": vec(at_q_gain), "at_k_gain": vec(at_k_gain),
        "at_qk_w_swap": jnp.dot(qkv_w[:, :, :QK_DIM], swap.astype(BF16), preferred_element_type=F32).astype(BF16),
        "at_qk_b_swap": vec(jnp.dot(at_qkv_b[:, :QK_DIM], swap, precision=hi)),
        "at_q_gain_swap": vec(jnp.dot(at_q_gain, swap_head, precision=hi)),
        "at_k_gain_swap": vec(jnp.dot(at_k_gain, swap_head, precision=hi)),
        "at_o_w": at_o_w.astype(BF16), "at_o_b": vec(at_o_b),
        "ff_in_w": ff_in_w.astype(BF16), "ff_in_b": vec(ff_in_b),
        "ff_conv_w": ff_conv_w, "ff_conv_b": vec(ff_conv_b),
        "ff_out_w": ff_out_w.astype(BF16), "ff_out_b": vec(ff_out_b),
    }
    rope = tuple(jnp.asarray(a) for a in _rope_tables(dec_seq))

    cond = jnp.concatenate([c_ctx[None, :], c, jnp.zeros((COND_ROWS - 1 - dec_batch, D_MODEL), F32)], 0)
    mods = _modulation_all(cond, w_mod, b_mod)

    past = cache_k.shape[2]
    ctx_k = cache_k.astype(BF16).reshape(dec_batch, n_attn, past, KV_DIM)
    ctx_v = cache_v.astype(BF16).reshape(dec_batch, n_attn, past, KV_DIM)

    y_p, new_k, new_v = _trunk(x_prompt.reshape(batch * seq, D_MODEL), mods, seq, False, W, None, None)
    y_s, _, _ = _trunk(x_sample.reshape(dec_batch * dec_seq, D_MODEL), mods, dec_seq, True, W, rope, (ctx_k, ctx_v))

    kv_shape = (batch, seq, N_KV_HEADS, HEAD_DIM)
    new_cache_k = jnp.stack([k.reshape(kv_shape) for k in new_k], axis=1)
    new_cache_v = jnp.stack([v.reshape(kv_shape) for v in new_v], axis=1)
    return (y_p.reshape(batch, seq, D_MODEL), y_s.reshape(dec_batch, dec_seq, D_MODEL), new_cache_k, new_cache_v)
```

```python
import functools
import math

import jax
import jax.numpy as jnp
import numpy as np
from jax import lax
from jax.experimental import pallas as pl
from jax.experimental.pallas import tpu as pltpu

D_MODEL = 1024
DEPTH = 4
GRID_W = 64
N_HEADS = 8
N_KV_HEADS = 2
HEAD_DIM = 128
GROUP = N_HEADS // N_KV_HEADS
Q_DIM = N_HEADS * HEAD_DIM
KV_DIM = N_KV_HEADS * HEAD_DIM
QK_DIM = Q_DIM + KV_DIM
QKV_DIM = Q_DIM + 2 * KV_DIM
ROPE_THETA = 10000.0
QK_EPS = 1e-6
POS_EMB_DIM = 33
FILTER_WIDTH = 64
N_INNER_MLPS = 2
FAST_DECAY_PCT = 0.3
SLOW_DECAY_PCT = 1.5
DECAY_TARGET = 1e-2
MOD_SHIFT = 0.0
D_FF = 2816
LN_EPS = 1e-5
N_MOD = 6
DN_ALPHA = (2 * DEPTH) ** 0.25

F32 = jnp.float32
BF16 = jnp.bfloat16

LANES = 128
SUBLANES = 8
MXU_WIDTH = 256
ROW_TILE = 1024
FF_CHUNK = MXU_WIDTH
HY_CHUNK = MXU_WIDTH
PHASES = 4
PHASE_ROWS = ROW_TILE // PHASES
GELU_K = math.sqrt(2.0 / math.pi)
LOG2_E = math.log2(math.e)
HEAD_STACK = 1
CTX_Q_TILE = 1024
MOD_COL_TILE = 1536
COND_ROWS = 16
FILTER_PAD = LANES
VMEM_LIMIT = 56 * 1024 * 1024


def _params(n_axes):
    return pltpu.CompilerParams(dimension_semantics=("arbitrary",) * n_axes,
                                vmem_limit_bytes=VMEM_LIMIT)


def _resident(shape):
    nd = len(shape)
    return pl.BlockSpec(shape, lambda *_: (0,) * nd, pipeline_mode=pl.Buffered(1))


def _layer_block(arr, *lead):
    shape = arr.shape[len(lead):]
    idx = tuple(lead) + (0,) * len(shape)
    return pl.BlockSpec((None,) * len(lead) + shape, lambda *_: idx, pipeline_mode=pl.Buffered(1))


def _mods_spec(layer, per_batch):
    if per_batch:
        return pl.BlockSpec((None, SUBLANES, N_MOD * D_MODEL), lambda i: (layer, (i + 1) // SUBLANES, 0))
    return pl.BlockSpec((None, SUBLANES, N_MOD * D_MODEL), lambda i: (layer, 0, 0))


def _mod_row(mod_ref, per_batch):
    row = (pl.program_id(0) + 1) % SUBLANES if per_batch else 0
    return mod_ref[pl.ds(row, 1), :]


def _mod(m, idx):
    return m[:, idx * D_MODEL:(idx + 1) * D_MODEL]


def _modulate(x, m, shift_idx):
    return x * (1.0 + _mod(m, shift_idx + 1)) + _mod(m, shift_idx)


def _residual_layer_norm(x, out, gate, g, b):
    y = DN_ALPHA * x + (1.0 + gate) * out
    mu = jnp.mean(y, axis=-1, keepdims=True)
    yc = y - mu
    var = jnp.mean(yc * yc, axis=-1, keepdims=True)
    return yc * lax.rsqrt(var + LN_EPS) * g + b


def _column_slab_specs():
    return [pl.BlockSpec((ROW_TILE, LANES), functools.partial(lambda j, i: (i, j), j)) for j in range(D_MODEL // LANES)]


def _read_phase(x_refs, p):
    return jnp.concatenate([r[pl.ds(p, PHASE_ROWS, stride=PHASES), :] for r in x_refs], axis=1)


def _slab_drain(slab_ref, o_ref):
    for j in range(D_MODEL // LANES):
        o_ref[:, j * LANES:(j + 1) * LANES] = slab_ref[j]


def _slab_write_phase(slab_ref, p, y):
    for j in range(D_MODEL // LANES):
        slab_ref[j, pl.ds(p, PHASE_ROWS, stride=PHASES), :] = y[:, j * LANES:(j + 1) * LANES]


def _slab_scratch():
    return pltpu.VMEM((D_MODEL // LANES, ROW_TILE, LANES), F32)


def _phase_bounds(seq_len):
    per_seq = seq_len // PHASES
    pos = lax.broadcasted_iota(jnp.int32, (PHASE_ROWS, 1), 0) & (per_seq - 1)
    return pos == 0, pos == per_seq - 1


def _phase_conv3(z, w, bias, conv_bias, seq_start, seq_end, scale=1.0):
    w = w * scale
    b_all = bias * (w[0:1] + w[1:2] + w[2:3]) + conv_bias * scale
    blk = [z[p * PHASE_ROWS:(p + 1) * PHASE_ROWS] for p in range(PHASES)]
    prev0 = jnp.where(seq_start, -bias, pltpu.roll(blk[PHASES - 1], 1, 0))
    next_last = jnp.where(seq_end, -bias, pltpu.roll(blk[0], PHASE_ROWS - 1, 0))
    out = []
    for p in range(PHASES):
        zp = prev0 if p == 0 else blk[p - 1]
        zn = next_last if p == PHASES - 1 else blk[p + 1]
        out.append(zp * w[0:1] + blk[p] * w[1:2] + zn * w[2:3] + b_all)
    return out


def _mod_kernel(cond_ref, w_ref, b_ref, o_ref):
    c = cond_ref[...]
    s = (c * jax.nn.sigmoid(c)).astype(BF16)
    o_ref[0] = jnp.dot(s, w_ref[0].astype(BF16), preferred_element_type=F32) + b_ref[0]


def _modulation_all(cond, w_mod, b_mod):
    n_col = (N_MOD * D_MODEL) // MOD_COL_TILE
    return pl.pallas_call(
        _mod_kernel,
        grid=(DEPTH, n_col),
        in_specs=[pl.BlockSpec((COND_ROWS, D_MODEL), lambda l, j: (0, 0)),
                  pl.BlockSpec((1, D_MODEL, MOD_COL_TILE), lambda l, j: (l, 0, j)),
                  pl.BlockSpec((1, 1, MOD_COL_TILE), lambda l, j: (l, 0, j))],
        out_specs=pl.BlockSpec((1, COND_ROWS, MOD_COL_TILE), lambda l, j: (l, 0, j)),
        out_shape=jax.ShapeDtypeStruct((DEPTH, COND_ROWS, N_MOD * D_MODEL), F32),
        compiler_params=_params(2),
        name="modulation",
    )(cond, w_mod, b_mod.reshape(DEPTH, 1, N_MOD * D_MODEL))


def _hyena_in_kernel(seq_len, per_batch, *refs):
    n_slab = D_MODEL // LANES
    x_refs, (mod_ref, w_ref, b_ref, sw_ref, sb_ref, x0_ref, u_ref, h_ref) = refs[:n_slab], refs[n_slab:]
    m = _mod_row(mod_ref, per_batch)
    for p in range(PHASES):
        h_ref[p * PHASE_ROWS:(p + 1) * PHASE_ROWS, :] = _modulate(_read_phase(x_refs, p), m, 0).astype(BF16)
    seq_start, seq_end = _phase_bounds(seq_len)
    for c in range(D_MODEL // HY_CHUNK):
        conv = []
        for part in range(3):
            cs = slice(part * D_MODEL + c * HY_CHUNK, part * D_MODEL + (c + 1) * HY_CHUNK)
            z = jnp.dot(h_ref[...], w_ref[:, cs], preferred_element_type=F32)
            conv.append(_phase_conv3(z, sw_ref[:, cs], b_ref[:, cs], sb_ref[:, cs], seq_start, seq_end))
        cs = slice(c * HY_CHUNK, (c + 1) * HY_CHUNK)
        for p, (x0, x1, v) in enumerate(zip(*conv)):
            rs = slice(p * PHASE_ROWS, (p + 1) * PHASE_ROWS)
            x0_ref[rs, cs] = x0.astype(BF16)
            u_ref[rs, cs] = (v * x1).astype(BF16)


def _hyena_in(x, mods, W, j, layer, seq_len, per_batch):
    t = x.shape[0]
    row = pl.BlockSpec((ROW_TILE, D_MODEL), lambda i: (i, 0))
    return pl.pallas_call(
        functools.partial(_hyena_in_kernel, seq_len, per_batch),
        grid=(t // ROW_TILE,),
        in_specs=_column_slab_specs() + [
            _mods_spec(layer, per_batch),
            _layer_block(W["hy_in_w"], j), _layer_block(W["hy_in_b"], j),
            _layer_block(W["hy_short_w"], j), _layer_block(W["hy_short_b"], j)],
        out_specs=[row, row],
        out_shape=[jax.ShapeDtypeStruct((t, D_MODEL), BF16)] * 2,
        scratch_shapes=[pltpu.VMEM((ROW_TILE, D_MODEL), BF16)],
        compiler_params=_params(1),
        name="hyena_in",
    )(*([x] * (D_MODEL // LANES)), mods, W["hy_in_w"], W["hy_in_b"], W["hy_short_w"], W["hy_short_b"])


def _hyena_filter_kernel(seq_len, z_ref, w1_ref, b1_ref, w2_ref, b2_ref, freq_ref, wo0_ref, wo1_ref,
                         t_ref, delta_ref, bias_ref, fm_ref, ha_ref, hb_ref, hc_ref, h_ref):
    hi = lax.Precision.HIGHEST

    @pl.when(pl.program_id(0) == 0)
    def _():
        freq = freq_ref[...]
        h = jnp.sin(freq * (jnp.dot(z_ref[...], w1_ref[...], precision=hi, preferred_element_type=F32)
                            + b1_ref[...]))
        for i in range(N_INNER_MLPS):
            h = jnp.sin(freq * (jnp.dot(h, w2_ref[i], precision=hi, preferred_element_type=F32) + b2_ref[i]))
        h_ref[...] = h

    h = h_ref[...]
    decay = jnp.exp(-t_ref[...] * delta_ref[...]) + MOD_SHIFT
    zero_rows = jnp.zeros((FILTER_PAD - FILTER_WIDTH, wo0_ref.shape[1]), F32)

    def taps(wo_ref):
        wo = jnp.concatenate([wo_ref[...], zero_rows], axis=0)
        return jnp.dot(h, wo, precision=hi, preferred_element_type=F32) * decay

    k0, k1 = taps(wo0_ref), taps(wo1_ref)
    row0 = lax.broadcasted_iota(jnp.int32, (seq_len, 1), 0) == 0
    k0 = jnp.where(row0, k0 + bias_ref[...], k0)
    k1 = jnp.where(row0, 0.0, k1)
    fm = fm_ref[...]
    s0 = jnp.dot(fm, k0.astype(BF16), preferred_element_type=F32)
    s1 = jnp.dot(fm, k1.astype(BF16), preferred_element_type=F32)
    ha = s0[:seq_len] + s1[:seq_len]
    s0b, s1b = s0[seq_len:], s1[seq_len:]
    nyq = s0b + s1b
    ha_ref[...] = ha
    hb_ref[...] = jnp.where(row0, 0.0, s0b - s1b)
    hc_ref[...] = jnp.where(row0, nyq, ha)


def _hyena_filter(consts, W, j, seq_len):
    feats, t_col, deltas, fm = consts
    dt = MXU_WIDTH
    n_d = D_MODEL // dt
    out = pl.BlockSpec((seq_len, dt), lambda i: (0, i))
    wout = W["hy_pos_wout"]
    return pl.pallas_call(
        functools.partial(_hyena_filter_kernel, seq_len),
        grid=(n_d,),
        in_specs=[_resident((seq_len, FILTER_PAD)),
                  _layer_block(W["hy_pos_w1"], j), _layer_block(W["hy_pos_b1"], j),
                  _layer_block(W["hy_pos_w2"], j), _layer_block(W["hy_pos_b2"], j),
                  _layer_block(W["hy_freq"], j),
                  pl.BlockSpec((None, FILTER_WIDTH, dt), lambda i: (j, 0, i)),
                  pl.BlockSpec((None, FILTER_WIDTH, dt), lambda i: (j, 0, n_d + i)),
                  _resident((seq_len, 1)), pl.BlockSpec((1, dt), lambda i: (0, i)),
                  pl.BlockSpec((None, 1, dt), lambda i: (j, 0, i)),
                  _resident((2 * seq_len, seq_len))],
        out_specs=[out, out, out],
        out_shape=[jax.ShapeDtypeStruct((seq_len, D_MODEL), F32)] * 3,
        scratch_shapes=[pltpu.VMEM((seq_len, FILTER_PAD), F32)],
        compiler_params=_params(1),
        name="hyena_filter",
    )(feats, W["hy_pos_w1"], W["hy_pos_b1"], W["hy_pos_w2"], W["hy_pos_b2"], W["hy_freq"], wout, wout,
      t_col, deltas, W["hy_filt_bias"], fm)


def _dft_conv_kernel(seq_len, u_ref, x0_ref, ha_ref, hb_ref, hc_ref, fm_ref, ga_ref, gb_ref, o_ref):
    ha, hb, hc = ha_ref[...], hb_ref[...], hc_ref[...]
    per = seq_len // PHASES
    for s in range(ROW_TILE // seq_len):
        rows = [slice(p * PHASE_ROWS + s * per, p * PHASE_ROWS + (s + 1) * per) for p in range(PHASES)]
        u = jnp.concatenate([u_ref[r, :] for r in rows], axis=0)
        spec = jnp.dot(fm_ref[...], u, preferred_element_type=F32)
        a, b = spec[:seq_len], spec[seq_len:]
        ya = (a * ha - b * hb).astype(BF16)
        yb = (a * hb + b * hc).astype(BF16)
        y = (jnp.dot(ga_ref[...], ya, preferred_element_type=F32)
             + jnp.dot(gb_ref[...], yb, preferred_element_type=F32))
        for p, r in enumerate(rows):
            o_ref[r, :] = (x0_ref[r, :].astype(F32) * y[p * per:(p + 1) * per]).astype(BF16)


def _dft_conv(u, x0, ha, hb, hc, fm, ga, gb, seq_len):
    t = u.shape[0]
    dt = 512 if seq_len == ROW_TILE else D_MODEL
    row = pl.BlockSpec((ROW_TILE, dt), lambda j, i: (i, j))
    filt = pl.BlockSpec((seq_len, dt), lambda j, i: (0, j))
    return pl.pallas_call(
        functools.partial(_dft_conv_kernel, seq_len),
        grid=(D_MODEL // dt, t // ROW_TILE),
        in_specs=[row, row, filt, filt, filt,
                  _resident((2 * seq_len, seq_len)), _resident((seq_len, seq_len)),
                  _resident((seq_len, seq_len))],
        out_specs=row,
        out_shape=jax.ShapeDtypeStruct((t, D_MODEL), BF16),
        compiler_params=_params(2),
        name="hyena_dft_conv",
    )(u, x0, ha, hb, hc, fm, ga, gb)


def _proj_ln_kernel(per_batch, phase_major, a_ref, x_ref, mod_ref, w_ref, b_ref, g_ref, beta_ref, o_ref, *scratch):
    out = jnp.dot(a_ref[...], w_ref[...], preferred_element_type=F32) + b_ref[...]
    if phase_major:
        (slab_ref,) = scratch
        for p in range(PHASES):
            _slab_write_phase(slab_ref, p, out[p * PHASE_ROWS:(p + 1) * PHASE_ROWS])
        out = jnp.concatenate([slab_ref[j] for j in range(D_MODEL // LANES)], axis=1)
    gate = _mod(_mod_row(mod_ref, per_batch), 2)
    o_ref[...] = _residual_layer_norm(x_ref[...], out, gate, g_ref[...], beta_ref[...])


def _proj_ln(a, x, mods, w, b, j, W, layer, per_batch, phase_major):
    t = x.shape[0]
    row = pl.BlockSpec((ROW_TILE, D_MODEL), lambda i: (i, 0))
    return pl.pallas_call(
        functools.partial(_proj_ln_kernel, per_batch, phase_major),
        grid=(t // ROW_TILE,),
        in_specs=[row, row, _mods_spec(layer, per_batch), _layer_block(w, j), _layer_block(b, j),
                  _layer_block(W["ln_g"], layer, 0), _layer_block(W["ln_b"], layer, 0)],
        out_specs=row,
        out_shape=jax.ShapeDtypeStruct((t, D_MODEL), F32),
        scratch_shapes=[_slab_scratch()] if phase_major else [],
        compiler_params=_params(1),
        name="proj_ln_phase" if phase_major else "proj_ln",
    )(a, x, mods, w, b, W["ln_g"], W["ln_b"])


def _ffn_kernel(seq_len, per_batch, *refs):
    n_slab = D_MODEL // LANES
    x_refs = refs[:n_slab]
    (mod_ref, wi_ref, bi_ref, cw_ref, cb_ref, wo_ref, bo_ref, g_ref, beta_ref,
     o_ref, acc_ref, h_ref, z_ref, slab_ref) = refs[n_slab:]
    m = _mod_row(mod_ref, per_batch)
    for p in range(PHASES):
        h_ref[p * PHASE_ROWS:(p + 1) * PHASE_ROWS, :] = _modulate(_read_phase(x_refs, p), m, 3).astype(BF16)
    seq_start, seq_end = _phase_bounds(seq_len)
    n_c = D_FF // FF_CHUNK

    def cols(c, part):
        return pl.ds(pl.multiple_of(part * D_FF + c * FF_CHUNK, FF_CHUNK), FF_CHUNK)

    def first_matmul(c, slot):
        for part in range(2):
            z_ref[slot, part] = jnp.dot(h_ref[...], wi_ref[:, cols(c, part)], preferred_element_type=F32)

    def gate_and_second_matmul(c, slot, init=False):
        conv = [_phase_conv3(z_ref[slot, part], cw_ref[:, cols(c, part)], bi_ref[:, cols(c, part)],
                             cb_ref[:, cols(c, part)], seq_start, seq_end, scale=(1.0, 0.5)[part])
                for part in range(2)]
        a = []
        for g, half_v in zip(*conv):
            t = jnp.tanh(g * (GELU_K + (GELU_K * 0.044715) * (g * g)))
            gv = g * half_v
            a.append((gv + gv * t).astype(BF16))
        a = jnp.concatenate(a, axis=0)
        row = pl.multiple_of(c * FF_CHUNK, FF_CHUNK)
        upd = jnp.dot(a, wo_ref[pl.ds(row, FF_CHUNK), :], preferred_element_type=F32)
        if init:
            acc_ref[...] = upd
        else:
            acc_ref[...] += upd

    first_matmul(0, 0)
    first_matmul(1, 1)
    gate_and_second_matmul(0, 0, init=True)

    def pair(i, carry):
        c = 2 * i + 1
        first_matmul(c + 1, 0)
        gate_and_second_matmul(c, 1)
        first_matmul(c + 2, 1)
        gate_and_second_matmul(c + 1, 0)
        return carry

    lax.fori_loop(0, (n_c - 3) // 2, pair, 0)
    first_matmul(n_c - 1, 0)
    gate_and_second_matmul(n_c - 2, 1)
    gate_and_second_matmul(n_c - 1, 0)
    for p in range(PHASES):
        out = acc_ref[p * PHASE_ROWS:(p + 1) * PHASE_ROWS, :] + bo_ref[...]
        y = _residual_layer_norm(_read_phase(x_refs, p), out, _mod(m, 5), g_ref[...], beta_ref[...])
        _slab_write_phase(slab_ref, p, y)
    _slab_drain(slab_ref, o_ref)


def _ffn(x, mods, W, layer, seq_len, per_batch):
    t = x.shape[0]
    assert (D_FF // FF_CHUNK) % 2 == 1
    row = pl.BlockSpec((ROW_TILE, D_MODEL), lambda i: (i, 0))
    names = ["ff_in_w", "ff_in_b", "ff_conv_w", "ff_conv_b", "ff_out_w", "ff_out_b"]
    return pl.pallas_call(
        functools.partial(_ffn_kernel, seq_len, per_batch),
        grid=(t // ROW_TILE,),
        in_specs=_column_slab_specs() + [_mods_spec(layer, per_batch)] + [_layer_block(W[n], layer) for n in names]
                 + [_layer_block(W["ln_g"], layer, 1), _layer_block(W["ln_b"], layer, 1)],
        out_specs=row,
        out_shape=jax.ShapeDtypeStruct((t, D_MODEL), F32),
        scratch_shapes=[pltpu.VMEM((ROW_TILE, D_MODEL), F32), pltpu.VMEM((ROW_TILE, D_MODEL), BF16),
                        pltpu.VMEM((2, 2, ROW_TILE, FF_CHUNK), F32), _slab_scratch()],
        compiler_params=_params(1),
        name="conv_ffn",
    )(*([x] * (D_MODEL // LANES)), mods, *[W[n] for n in names], W["ln_g"], W["ln_b"])


def _qkv_kernel(rope, per_batch, x_ref, mod_ref, w_ref, b_ref, qg_ref, kg_ref, *rest):
    scale = LOG2_E * HEAD_DIM ** -0.5
    h = _modulate(x_ref[...], _mod_row(mod_ref, per_batch), 0).astype(BF16)
    z = jnp.dot(h, w_ref[...], preferred_element_type=F32) + b_ref[...]
    if rope:
        ws_ref, bs_ref, qgs_ref, kgs_ref, cos_ref, sin_ref, q_ref, k_ref, v_ref = rest
        zp = jnp.dot(h, ws_ref[...], preferred_element_type=F32) + bs_ref[...]
        cos, sin = cos_ref[...], sin_ref[...]
        tables = {"q": (qg_ref[...] * scale * cos, qgs_ref[...] * scale * sin),
                  "k": (kg_ref[...] * cos, kgs_ref[...] * sin)}
    else:
        q_ref, k_ref, v_ref = rest

    def head(col, kind):
        zi = z[:, col:col + HEAD_DIM]
        inv = lax.rsqrt(jnp.mean(zi * zi, axis=-1, keepdims=True) + QK_EPS)
        if rope:
            gc, gs = tables[kind]
            return inv * (zi * gc + zp[:, col:col + HEAD_DIM] * gs)
        if kind == "q":
            return zi * inv * (qg_ref[...] * scale)
        return zi * inv * kg_ref[...]

    for i in range(N_HEADS):
        q_ref[:, i * HEAD_DIM:(i + 1) * HEAD_DIM] = head(i * HEAD_DIM, "q").astype(q_ref.dtype)
    for i in range(N_KV_HEADS):
        cols = slice(i * HEAD_DIM, (i + 1) * HEAD_DIM)
        k_head = head(Q_DIM + i * HEAD_DIM, "k")
        v_head = z[:, QK_DIM + i * HEAD_DIM:QK_DIM + (i + 1) * HEAD_DIM]
        if k_ref.shape[1] == HEAD_DIM:
            rows = pl.ds(i, ROW_TILE, stride=N_KV_HEADS)
            k_ref[rows, :] = k_head.astype(k_ref.dtype)
            v_ref[rows, :] = v_head.astype(v_ref.dtype)
        else:
            k_ref[:, cols] = k_head.astype(k_ref.dtype)
            v_ref[:, cols] = v_head.astype(v_ref.dtype)


def _qkv(x, mods, W, j, layer, rope_tables, kv_dtype, per_batch):
    t = x.shape[0]
    rope = rope_tables is not None
    row = pl.BlockSpec((ROW_TILE, D_MODEL), lambda i: (i, 0))
    if kv_dtype == F32:
        kv_shape = (t * N_KV_HEADS, HEAD_DIM)
        kv_row = pl.BlockSpec((ROW_TILE * N_KV_HEADS, HEAD_DIM), lambda i: (i, 0))
    else:
        kv_shape = (t, KV_DIM)
        kv_row = pl.BlockSpec((ROW_TILE, KV_DIM), lambda i: (i, 0))
    names = ["at_qkv_w", "at_qkv_b", "at_q_gain", "at_k_gain"]
    if rope:
        names += ["at_qk_w_swap", "at_qk_b_swap", "at_q_gain_swap", "at_k_gain_swap"]
    in_specs = [row, _mods_spec(layer, per_batch)] + [_layer_block(W[n], j) for n in names]
    args = [x, mods] + [W[n] for n in names]
    if rope:
        in_specs += [_resident((ROW_TILE, HEAD_DIM))] * 2
        args += list(rope_tables)
    return pl.pallas_call(
        functools.partial(_qkv_kernel, rope, per_batch),
        grid=(t // ROW_TILE,),
        in_specs=in_specs,
        out_specs=[row, kv_row, kv_row],
        out_shape=[jax.ShapeDtypeStruct((t, Q_DIM), BF16), jax.ShapeDtypeStruct(kv_shape, kv_dtype),
                   jax.ShapeDtypeStruct(kv_shape, kv_dtype)],
        compiler_params=_params(1),
        name="qkv_rope" if rope else "qkv",
    )(*args)


def _attn_kernel(has_ctx, seq_len, q_ref, k_ref, v_ref, *rest):
    if has_ctx:
        kc_ref, vc_ref, o_ref = rest
        kc = kc_ref[...]
        vc = jnp.concatenate([vc_ref[...], jnp.ones(vc_ref.shape, BF16)], axis=1)
    else:
        (o_ref,) = rest
    nt = (((1,), (1,)), ((), ()))
    heads_in_block = N_KV_HEADS if k_ref.dtype == F32 else 1
    n_seq = k_ref.shape[0] // (seq_len * heads_in_block)
    q_rows = q_ref.shape[0] // n_seq
    ones = jnp.ones((seq_len, HEAD_DIM), BF16)
    for s in range(n_seq):
        if heads_in_block > 1:
            kv_rows = pl.ds(s * seq_len * heads_in_block + pl.program_id(1), seq_len, stride=heads_in_block)
        else:
            kv_rows = slice(s * seq_len, (s + 1) * seq_len)
        k = k_ref[kv_rows, :].astype(BF16)
        v = jnp.concatenate([v_ref[kv_rows, :].astype(BF16), ones], axis=1)
        rows = slice(s * q_rows, (s + 1) * q_rows)
        for g0 in range(0, GROUP, HEAD_STACK):
            heads = range(g0, g0 + HEAD_STACK)
            q = jnp.concatenate([q_ref[rows, g * HEAD_DIM:(g + 1) * HEAD_DIM] for g in heads], axis=0)
            sc_new = lax.dot_general(q, k, nt, preferred_element_type=F32)
            mx = jnp.max(sc_new, axis=-1, keepdims=True)
            if has_ctx:
                sc_ctx = lax.dot_general(q, kc, nt, preferred_element_type=F32)
                mx = jnp.maximum(mx, jnp.max(sc_ctx, axis=-1, keepdims=True))
            o = jnp.dot(jnp.exp2(sc_new - mx).astype(BF16), v, preferred_element_type=F32)
            if has_ctx:
                o = o + jnp.dot(jnp.exp2(sc_ctx - mx).astype(BF16), vc, preferred_element_type=F32)
            o = (o[:, :HEAD_DIM] / o[:, HEAD_DIM:]).astype(BF16)
            for n, g in enumerate(heads):
                o_ref[rows, g * HEAD_DIM:(g + 1) * HEAD_DIM] = o[n * q_rows:(n + 1) * q_rows]


def _attention(q, k, v, ctx, seq_len, q_tile):
    t = q.shape[0]
    gw = GROUP * HEAD_DIM
    kv_rows = max(seq_len, q_tile)
    n_b = t // kv_rows
    n_q = kv_rows // q_tile
    q_spec = pl.BlockSpec((q_tile, gw), lambda b, h, i: (b * n_q + i, h))
    if k.dtype == F32:
        kv_spec = pl.BlockSpec((kv_rows * N_KV_HEADS, HEAD_DIM), lambda b, h, i: (b, 0))
    else:
        kv_spec = pl.BlockSpec((kv_rows, HEAD_DIM), lambda b, h, i: (b, h))
    in_specs = [q_spec, kv_spec, kv_spec]
    args = [q, k, v]
    if ctx is not None:
        kc, vc, layer = ctx
        ctx_spec = pl.BlockSpec((None, None, kc.shape[2], HEAD_DIM), lambda b, h, i: (b, layer, 0, h))
        in_specs += [ctx_spec, ctx_spec]
        args += [kc, vc]
    return pl.pallas_call(
        functools.partial(_attn_kernel, ctx is not None, seq_len),
        grid=(n_b, N_KV_HEADS, n_q),
        in_specs=in_specs,
        out_specs=q_spec,
        out_shape=jax.ShapeDtypeStruct((t, Q_DIM), BF16),
        compiler_params=_params(3),
        name="attention_ctx" if ctx is not None else "attention",
    )(*args)


def _hyena_feats(seq_len):
    t = np.linspace(0.0, 1.0, seq_len)[:, None]
    n_bands = (POS_EMB_DIM - 1) // 2
    w = 2.0 * math.pi * np.arange(seq_len) / seq_len
    f = np.linspace(1e-4, n_bands - 1, n_bands)
    ang = w[:, None] * f[None, :]
    z = np.concatenate([t, np.cos(ang), -np.sin(ang)], -1)
    z = np.pad(z, ((0, 0), (0, FILTER_PAD - POS_EMB_DIM)))
    max_decay = math.log(DECAY_TARGET) / FAST_DECAY_PCT
    min_decay = math.log(DECAY_TARGET) / SLOW_DECAY_PCT
    deltas = np.abs(np.linspace(min_decay, max_decay, D_MODEL))[None, :]
    return z.astype(np.float32), t.astype(np.float32), deltas.astype(np.float32)


def _dft_mats(seq_len):
    n = 2 * seq_len
    idx = np.arange(seq_len)
    ang = 2.0 * math.pi * ((idx[:, None] * idx[None, :]) % n) / n
    sign = np.where(idx % 2 == 0, 1.0, -1.0)
    fa, fb = np.cos(ang), -np.sin(ang)
    fb[0, :] = sign
    fm = np.concatenate([fa, fb], 0)
    ga, gb = (2.0 / n) * np.cos(ang), -(2.0 / n) * np.sin(ang)
    ga[:, 0] = 1.0 / n
    gb[:, 0] = sign / n
    time_of = (np.arange(seq_len) % (seq_len // PHASES)) * PHASES + np.arange(seq_len) // (seq_len // PHASES)
    return (fm.astype(np.float32), fm[:, time_of].astype(np.float32),
            ga[time_of, :].astype(np.float32), gb[time_of, :].astype(np.float32))


def _rope_tables(seq_len):
    rows = np.repeat(np.arange(seq_len // GRID_W), GRID_W).astype(np.float64)
    cols = np.tile(np.arange(GRID_W), seq_len // GRID_W).astype(np.float64)
    half = HEAD_DIM // 2
    inv = ROPE_THETA ** (-np.arange(0, half, 2, dtype=np.float64) / half)
    ang = np.concatenate([rows[:, None] * inv, cols[:, None] * inv], -1)
    cos = np.repeat(np.cos(ang), 2, axis=-1)
    sin = np.stack([-np.sin(ang), np.sin(ang)], axis=-1).reshape(seq_len, HEAD_DIM)
    return cos.astype(np.float32), sin.astype(np.float32)


def _pair_swap_matrix(n):
    p = np.zeros((n, n), np.float32)
    idx = np.arange(n)
    p[idx ^ 1, idx] = 1.0
    return p


def _trunk(x, mods, seq_len, per_batch, W, rope, ctx):
    fm, fm_pm, ga_pm, gb_pm = (jnp.asarray(a).astype(BF16) for a in _dft_mats(seq_len))
    feats, t_col, deltas = (jnp.asarray(a) for a in _hyena_feats(seq_len))
    new_k, new_v = [], []
    for l in range(DEPTH):
        j = l // 2
        if l % 2 == 0:
            ha, hb, hc = _hyena_filter((feats, t_col, deltas, fm), W, j, seq_len)
            x0, u = _hyena_in(x, mods, W, j, l, seq_len, per_batch)
            a = _dft_conv(u, x0, ha, hb, hc, fm_pm, ga_pm, gb_pm, seq_len)
            w_o, b_o = W["hy_out_w"], W["hy_out_b"]
        else:
            if ctx is None:
                q, k, v = _qkv(x, mods, W, j, l, None, F32, per_batch)
                new_k.append(k)
                new_v.append(v)
                a = _attention(q, k, v, None, seq_len, ROW_TILE)
            else:
                q, k, v = _qkv(x, mods, W, j, l, rope, BF16, per_batch)
                a = _attention(q, k, v, (ctx[0], ctx[1], j), seq_len, CTX_Q_TILE)
            w_o, b_o = W["at_o_w"], W["at_o_b"]
        x = _proj_ln(a, x, mods, w_o, b_o, j, W, l, per_batch, l % 2 == 0)
        x = _ffn(x, mods, W, l, seq_len, per_batch)
    return x, new_k, new_v


def kernel(x_prompt, x_sample, cache_k, cache_v, c, c_ctx, w_mod, b_mod, ln_g, ln_b, hy_in_w, hy_in_b, hy_short_w, hy_short_b, hy_pos_w1, hy_pos_b1, hy_pos_w2, hy_pos_b2, hy_pos_wout, hy_freq, hy_filt_bias, hy_out_w, hy_out_b, at_qkv_w, at_qkv_b, at_q_gain, at_k_gain, at_o_w, at_o_b, ff_in_w, ff_in_b, ff_conv_w, ff_conv_b, ff_out_w, ff_out_b):
    batch, seq, _ = x_prompt.shape
    dec_batch, dec_seq, _ = x_sample.shape
    assert dec_seq == ROW_TILE and ROW_TILE % seq == 0 and 1 + dec_batch <= COND_ROWS
    n_attn = at_qkv_w.shape[0]
    pad = FILTER_PAD - FILTER_WIDTH

    def vec(a):
        return a[:, None, :]

    qkv_w = at_qkv_w.astype(BF16)
    swap = jnp.asarray(_pair_swap_matrix(QK_DIM))
    swap_head = jnp.asarray(_pair_swap_matrix(HEAD_DIM))
    hi = lax.Precision.HIGHEST
    W = {
        "ln_g": ln_g[:, :, None, :], "ln_b": ln_b[:, :, None, :],
        "hy_in_w": hy_in_w.astype(BF16), "hy_in_b": vec(hy_in_b),
        "hy_short_w": hy_short_w, "hy_short_b": vec(hy_short_b),
        "hy_pos_w1": jnp.pad(hy_pos_w1, ((0, 0), (0, FILTER_PAD - POS_EMB_DIM), (0, pad))),
        "hy_pos_b1": vec(jnp.pad(hy_pos_b1, ((0, 0), (0, pad)))),
        "hy_pos_w2": jnp.pad(hy_pos_w2, ((0, 0), (0, 0), (0, pad), (0, pad))),
        "hy_pos_b2": jnp.pad(hy_pos_b2, ((0, 0), (0, 0), (0, pad)))[:, :, None, :],
        "hy_freq": vec(jnp.pad(hy_freq, ((0, 0), (0, pad)))),
        "hy_pos_wout": hy_pos_wout, "hy_filt_bias": vec(hy_filt_bias),
        "hy_out_w": hy_out_w.astype(BF16), "hy_out_b": vec(hy_out_b),
        "at_qkv_w": qkv_w, "at_qkv_b": vec(at_qkv_b),
        "at_q_gain": vec(at_q_gain), "at_k_gain": vec(at_k_gain),
        "at_qk_w_swap": jnp.dot(qkv_w[:, :, :QK_DIM], swap.astype(BF16), preferred_element_type=F32).astype(BF16),
        "at_qk_b_swap": vec(jnp.dot(at_qkv_b[:, :QK_DIM], swap, precision=hi)),
        "at_q_gain_swap": vec(jnp.dot(at_q_gain, swap_head, precision=hi)),
        "at_k_gain_swap": vec(jnp.dot(at_k_gain, swap_head, precision=hi)),
        "at_o_w": at_o_w.astype(BF16), "at_o_b": vec(at_o_b),
        "ff_in_w": ff_in_w.astype(BF16), "ff_in_b": vec(ff_in_b),
        "ff_conv_w": ff_conv_w, "ff_conv_b": vec(ff_conv_b),
        "ff_out_w": ff_out_w.astype(BF16), "ff_out_b": vec(ff_out_b),
    }
    rope = tuple(jnp.asarray(a) for a in _rope_tables(dec_seq))

    cond = jnp.concatenate([c_ctx[None, :], c, jnp.zeros((COND_ROWS - 1 - dec_batch, D_MODEL), F32)], 0)
    mods = _modulation_all(cond, w_mod, b_mod)

    past = cache_k.shape[2]
    ctx_k = cache_k.astype(BF16).reshape(dec_batch, n_attn, past, KV_DIM)
    ctx_v = cache_v.astype(BF16).reshape(dec_batch, n_attn, past, KV_DIM)

    y_p, new_k, new_v = _trunk(x_prompt.reshape(batch * seq, D_MODEL), mods, seq, False, W, None, None)
    y_s, _, _ = _trunk(x_sample.reshape(dec_batch * dec_seq, D_MODEL), mods, dec_seq, True, W, rope, (ctx_k, ctx_v))

    kv_shape = (batch, seq, N_KV_HEADS, HEAD_DIM)
    new_cache_k = jnp.stack([k.reshape(kv_shape) for k in new_k], axis=1)
    new_cache_v = jnp.stack([v.reshape(kv_shape) for v in new_v], axis=1)
    return (y_p.reshape(batch, seq, D_MODEL), y_s.reshape(dec_batch, dec_seq, D_MODEL), new_cache_k, new_cache_v)
```

```python
import functools
import math

import jax
import jax.numpy as jnp
import numpy as np
from jax import lax
from jax.experimental import pallas as pl
from jax.experimental.pallas import tpu as pltpu

D_MODEL = 1024
DEPTH = 4
GRID_W = 64
N_HEADS = 8
N_KV_HEADS = 2
HEAD_DIM = 128
GROUP = N_HEADS // N_KV_HEADS
Q_DIM = N_HEADS * HEAD_DIM
KV_DIM = N_KV_HEADS * HEAD_DIM
QK_DIM = Q_DIM + KV_DIM
QKV_DIM = Q_DIM + 2 * KV_DIM
ROPE_THETA = 10000.0
QK_EPS = 1e-6
POS_EMB_DIM = 33
FILTER_WIDTH = 64
N_INNER_MLPS = 2
FAST_DECAY_PCT = 0.3
SLOW_DECAY_PCT = 1.5
DECAY_TARGET = 1e-2
MOD_SHIFT = 0.0
D_FF = 2816
LN_EPS = 1e-5
N_MOD = 6
DN_ALPHA = (2 * DEPTH) ** 0.25

F32 = jnp.float32
BF16 = jnp.bfloat16

LANES = 128
SUBLANES = 8
MXU_WIDTH = 256
ROW_TILE = 1024
FF_CHUNK = MXU_WIDTH
HY_CHUNK = MXU_WIDTH
PHASES = 4
PHASE_ROWS = ROW_TILE // PHASES
GELU_K = math.sqrt(2.0 / math.pi)
LOG2_E = math.log2(math.e)
HEAD_STACK = 1
CTX_Q_TILE = 1024
MOD_COL_TILE = 1536
COND_ROWS = 16
FILTER_PAD = LANES
VMEM_LIMIT = 56 * 1024 * 1024


def _params(n_axes):
    return pltpu.CompilerParams(dimension_semantics=("arbitrary",) * n_axes,
                                vmem_limit_bytes=VMEM_LIMIT)


def _resident(shape):
    nd = len(shape)
    return pl.BlockSpec(shape, lambda *_: (0,) * nd, pipeline_mode=pl.Buffered(1))


def _layer_block(arr, *lead):
    shape = arr.shape[len(lead):]
    idx = tuple(lead) + (0,) * len(shape)
    return pl.BlockSpec((None,) * len(lead) + shape, lambda *_: idx, pipeline_mode=pl.Buffered(1))


def _mods_spec(layer, per_batch):
    if per_batch:
        return pl.BlockSpec((None, SUBLANES, N_MOD * D_MODEL), lambda i: (layer, (i + 1) // SUBLANES, 0))
    return pl.BlockSpec((None, SUBLANES, N_MOD * D_MODEL), lambda i: (layer, 0, 0))


def _mod_row(mod_ref, per_batch):
    row = (pl.program_id(0) + 1) % SUBLANES if per_batch else 0
    return mod_ref[pl.ds(row, 1), :]


def _mod(m, idx):
    return m[:, idx * D_MODEL:(idx + 1) * D_MODEL]


def _modulate(x, m, shift_idx):
    return x * (1.0 + _mod(m, shift_idx + 1)) + _mod(m, shift_idx)


def _residual_layer_norm(x, out, gate, g, b):
    y = DN_ALPHA * x + (1.0 + gate) * out
    mu = jnp.mean(y, axis=-1, keepdims=True)
    yc = y - mu
    var = jnp.mean(yc * yc, axis=-1, keepdims=True)
    return yc * lax.rsqrt(var + LN_EPS) * g + b


def _column_slab_specs():
    return [pl.BlockSpec((ROW_TILE, LANES), functools.partial(lambda j, i: (i, j), j)) for j in range(D_MODEL // LANES)]


def _read_phase(x_refs, p):
    return jnp.concatenate([r[pl.ds(p, PHASE_ROWS, stride=PHASES), :] for r in x_refs], axis=1)


def _slab_drain(slab_ref, o_ref):
    for j in range(D_MODEL // LANES):
        o_ref[:, j * LANES:(j + 1) * LANES] = slab_ref[j]


def _slab_write_phase(slab_ref, p, y):
    for j in range(D_MODEL // LANES):
        slab_ref[j, pl.ds(p, PHASE_ROWS, stride=PHASES), :] = y[:, j * LANES:(j + 1) * LANES]


def _slab_scratch():
    return pltpu.VMEM((D_MODEL // LANES, ROW_TILE, LANES), F32)


def _phase_bounds(seq_len):
    per_seq = seq_len // PHASES
    pos = lax.broadcasted_iota(jnp.int32, (PHASE_ROWS, 1), 0) & (per_seq - 1)
    return pos == 0, pos == per_seq - 1


def _phase_conv3(z, w, bias, conv_bias, seq_start, seq_end, scale=1.0):
    w = w * scale
    b_all = bias * (w[0:1] + w[1:2] + w[2:3]) + conv_bias * scale
    blk = [z[p * PHASE_ROWS:(p + 1) * PHASE_ROWS] for p in range(PHASES)]
    prev0 = jnp.where(seq_start, -bias, pltpu.roll(blk[PHASES - 1], 1, 0))
    next_last = jnp.where(seq_end, -bias, pltpu.roll(blk[0], PHASE_ROWS - 1, 0))
    out = []
    for p in range(PHASES):
        zp = prev0 if p == 0 else blk[p - 1]
        zn = next_last if p == PHASES - 1 else blk[p + 1]
        out.append(zp * w[0:1] + blk[p] * w[1:2] + zn * w[2:3] + b_all)
    return out


def _mod_kernel(cond_ref, w_ref, b_ref, o_ref):
    c = cond_ref[...]
    s = (c * jax.nn.sigmoid(c)).astype(BF16)
    o_ref[0] = jnp.dot(s, w_ref[0].astype(BF16), preferred_element_type=F32) + b_ref[0]


def _modulation_all(cond, w_mod, b_mod):
    n_col = (N_MOD * D_MODEL) // MOD_COL_TILE
    return pl.pallas_call(
        _mod_kernel,
        grid=(DEPTH, n_col),
        in_specs=[pl.BlockSpec((COND_ROWS, D_MODEL), lambda l, j: (0, 0)),
                  pl.BlockSpec((1, D_MODEL, MOD_COL_TILE), lambda l, j: (l, 0, j)),
                  pl.BlockSpec((1, 1, MOD_COL_TILE), lambda l, j: (l, 0, j))],
        out_specs=pl.BlockSpec((1, COND_ROWS, MOD_COL_TILE), lambda l, j: (l, 0, j)),
        out_shape=jax.ShapeDtypeStruct((DEPTH, COND_ROWS, N_MOD * D_MODEL), F32),
        compiler_params=_params(2),
        name="modulation",
    )(cond, w_mod, b_mod.reshape(DEPTH, 1, N_MOD * D_MODEL))


def _hyena_in_kernel(seq_len, per_batch, *refs):
    n_slab = D_MODEL // LANES
    x_refs, (mod_ref, w_ref, b_ref, sw_ref, sb_ref, x0_ref, u_ref, h_ref) = refs[:n_slab], refs[n_slab:]
    m = _mod_row(mod_ref, per_batch)
    for p in range(PHASES):
        h_ref[p * PHASE_ROWS:(p + 1) * PHASE_ROWS, :] = _modulate(_read_phase(x_refs, p), m, 0).astype(BF16)
    seq_start, seq_end = _phase_bounds(seq_len)
    for c in range(D_MODEL // HY_CHUNK):
        conv = []
        for part in range(3):
            cs = slice(part * D_MODEL + c * HY_CHUNK, part * D_MODEL + (c + 1) * HY_CHUNK)
            z = jnp.dot(h_ref[...], w_ref[:, cs], preferred_element_type=F32)
            conv.append(_phase_conv3(z, sw_ref[:, cs], b_ref[:, cs], sb_ref[:, cs], seq_start, seq_end))
        cs = slice(c * HY_CHUNK, (c + 1) * HY_CHUNK)
        for p, (x0, x1, v) in enumerate(zip(*conv)):
            rs = slice(p * PHASE_ROWS, (p + 1) * PHASE_ROWS)
            x0_ref[rs, cs] = x0.astype(BF16)
            u_ref[rs, cs] = (v * x1).astype(BF16)


def _hyena_in(x, mods, W, j, layer, seq_len, per_batch):
    t = x.shape[0]
    row = pl.BlockSpec((ROW_TILE, D_MODEL), lambda i: (i, 0))
    return pl.pallas_call(
        functools.partial(_hyena_in_kernel, seq_len, per_batch),
        grid=(t // ROW_TILE,),
        in_specs=_column_slab_specs() + [
            _mods_spec(layer, per_batch),
            _layer_block(W["hy_in_w"], j), _layer_block(W["hy_in_b"], j),
            _layer_block(W["hy_short_w"], j), _layer_block(W["hy_short_b"], j)],
        out_specs=[row, row],
        out_shape=[jax.ShapeDtypeStruct((t, D_MODEL), BF16)] * 2,
        scratch_shapes=[pltpu.VMEM((ROW_TILE, D_MODEL), BF16)],
        compiler_params=_params(1),
        name="hyena_in",
    )(*([x] * (D_MODEL // LANES)), mods, W["hy_in_w"], W["hy_in_b"], W["hy_short_w"], W["hy_short_b"])


def _hyena_filter_kernel(seq_len, z_ref, w1_ref, b1_ref, w2_ref, b2_ref, freq_ref, wo0_ref, wo1_ref,
                         t_ref, delta_ref, bias_ref, fm_ref, ha_ref, hb_ref, hc_ref, h_ref):
    hi = lax.Precision.HIGHEST

    @pl.when(pl.program_id(0) == 0)
    def _():
        freq = freq_ref[...]
        h = jnp.sin(freq * (jnp.dot(z_ref[...], w1_ref[...], precision=hi, preferred_element_type=F32)
                            + b1_ref[...]))
        for i in range(N_INNER_MLPS):
            h = jnp.sin(freq * (jnp.dot(h, w2_ref[i], precision=hi, preferred_element_type=F32) + b2_ref[i]))
        h_ref[...] = h

    h = h_ref[...]
    decay = jnp.exp(-t_ref[...] * delta_ref[...]) + MOD_SHIFT
    zero_rows = jnp.zeros((FILTER_PAD - FILTER_WIDTH, wo0_ref.shape[1]), F32)

    def taps(wo_ref):
        wo = jnp.concatenate([wo_ref[...], zero_rows], axis=0)
        return jnp.dot(h, wo, precision=hi, preferred_element_type=F32) * decay

    k0, k1 = taps(wo0_ref), taps(wo1_ref)
    row0 = lax.broadcasted_iota(jnp.int32, (seq_len, 1), 0) == 0
    k0 = jnp.where(row0, k0 + bias_ref[...], k0)
    k1 = jnp.where(row0, 0.0, k1)
    fm = fm_ref[...]
    s0 = jnp.dot(fm, k0.astype(BF16), preferred_element_type=F32)
    s1 = jnp.dot(fm, k1.astype(BF16), preferred_element_type=F32)
    ha = s0[:seq_len] + s1[:seq_len]
    s0b, s1b = s0[seq_len:], s1[seq_len:]
    nyq = s0b + s1b
    ha_ref[...] = ha
    hb_ref[...] = jnp.where(row0, 0.0, s0b - s1b)
    hc_ref[...] = jnp.where(row0, nyq, ha)


def _hyena_filter(consts, W, j, seq_len):
    feats, t_col, deltas, fm = consts
    dt = MXU_WIDTH
    n_d = D_MODEL // dt
    out = pl.BlockSpec((seq_len, dt), lambda i: (0, i))
    wout = W["hy_pos_wout"]
    return pl.pallas_call(
        functools.partial(_hyena_filter_kernel, seq_len),
        grid=(n_d,),
        in_specs=[_resident((seq_len, FILTER_PAD)),
                  _layer_block(W["hy_pos_w1"], j), _layer_block(W["hy_pos_b1"], j),
                  _layer_block(W["hy_pos_w2"], j), _layer_block(W["hy_pos_b2"], j),
                  _layer_block(W["hy_freq"], j),
                  pl.BlockSpec((None, FILTER_WIDTH, dt), lambda i: (j, 0, i)),
                  pl.BlockSpec((None, FILTER_WIDTH, dt), lambda i: (j, 0, n_d + i)),
                  _resident((seq_len, 1)), pl.BlockSpec((1, dt), lambda i: (0, i)),
                  pl.BlockSpec((None, 1, dt), lambda i: (j, 0, i)),
                  _resident((2 * seq_len, seq_len))],
        out_specs=[out, out, out],
        out_shape=[jax.ShapeDtypeStruct((seq_len, D_MODEL), F32)] * 3,
        scratch_shapes=[pltpu.VMEM((seq_len, FILTER_PAD), F32)],
        compiler_params=_params(1),
        name="hyena_filter",
    )(feats, W["hy_pos_w1"], W["hy_pos_b1"], W["hy_pos_w2"], W["hy_pos_b2"], W["hy_freq"], wout, wout,
      t_col, deltas, W["hy_filt_bias"], fm)


def _dft_conv_kernel(seq_len, u_ref, x0_ref, ha_ref, hb_ref, hc_ref, fm_ref, ga_ref, gb_ref, o_ref):
    ha, hb, hc = ha_ref[...], hb_ref[...], hc_ref[...]
    per = seq_len // PHASES
    for s in range(ROW_TILE // seq_len):
        rows = [slice(p * PHASE_ROWS + s * per, p * PHASE_ROWS + (s + 1) * per) for p in range(PHASES)]
        u = jnp.concatenate([u_ref[r, :] for r in rows], axis=0)
        spec = jnp.dot(fm_ref[...], u, preferred_element_type=F32)
        a, b = spec[:seq_len], spec[seq_len:]
        ya = (a * ha - b * hb).astype(BF16)
        yb = (a * hb + b * hc).astype(BF16)
        y = (jnp.dot(ga_ref[...], ya, preferred_element_type=F32)
             + jnp.dot(gb_ref[...], yb, preferred_element_type=F32))
        for p, r in enumerate(rows):
            o_ref[r, :] = (x0_ref[r, :].astype(F32) * y[p * per:(p + 1) * per]).astype(BF16)


def _dft_conv(u, x0, ha, hb, hc, fm, ga, gb, seq_len):
    t = u.shape[0]
    dt = 512 if seq_len == ROW_TILE else D_MODEL
    row = pl.BlockSpec((ROW_TILE, dt), lambda j, i: (i, j))
    filt = pl.BlockSpec((seq_len, dt), lambda j, i: (0, j))
    return pl.pallas_call(
        functools.partial(_dft_conv_kernel, seq_len),
        grid=(D_MODEL // dt, t // ROW_TILE),
        in_specs=[row, row, filt, filt, filt,
                  _resident((2 * seq_len, seq_len)), _resident((seq_len, seq_len)),
                  _resident((seq_len, seq_len))],
        out_specs=row,
        out_shape=jax.ShapeDtypeStruct((t, D_MODEL), BF16),
        compiler_params=_params(2),
        name="hyena_dft_conv",
    )(u, x0, ha, hb, hc, fm, ga, gb)


def _proj_ln_kernel(per_batch, phase_major, a_ref, x_ref, mod_ref, w_ref, b_ref, g_ref, beta_ref, o_ref, *scratch):
    gate = _mod(_mod_row(mod_ref, per_batch), 2)
    if not phase_major:
        for r in range(PHASES):
            rs = slice(r * PHASE_ROWS, (r + 1) * PHASE_ROWS)
            out = jnp.dot(a_ref[rs, :], w_ref[...], preferred_element_type=F32) + b_ref[...]
            o_ref[rs, :] = _residual_layer_norm(x_ref[rs, :], out, gate, g_ref[...], beta_ref[...])
        return
    out = jnp.dot(a_ref[...], w_ref[...], preferred_element_type=F32) + b_ref[...]
    if phase_major:
        (slab_ref,) = scratch
        for p in range(PHASES):
            _slab_write_phase(slab_ref, p, out[p * PHASE_ROWS:(p + 1) * PHASE_ROWS])
        out = jnp.concatenate([slab_ref[j] for j in range(D_MODEL // LANES)], axis=1)
    gate = _mod(_mod_row(mod_ref, per_batch), 2)
    o_ref[...] = _residual_layer_norm(x_ref[...], out, gate, g_ref[...], beta_ref[...])


def _proj_ln(a, x, mods, w, b, j, W, layer, per_batch, phase_major):
    t = x.shape[0]
    row = pl.BlockSpec((ROW_TILE, D_MODEL), lambda i: (i, 0))
    return pl.pallas_call(
        functools.partial(_proj_ln_kernel, per_batch, phase_major),
        grid=(t // ROW_TILE,),
        in_specs=[row, row, _mods_spec(layer, per_batch), _layer_block(w, j), _layer_block(b, j),
                  _layer_block(W["ln_g"], layer, 0), _layer_block(W["ln_b"], layer, 0)],
        out_specs=row,
        out_shape=jax.ShapeDtypeStruct((t, D_MODEL), F32),
        scratch_shapes=[_slab_scratch()] if phase_major else [],
        compiler_params=_params(1),
        name="proj_ln_phase" if phase_major else "proj_ln",
    )(a, x, mods, w, b, W["ln_g"], W["ln_b"])


def _ffn_kernel(seq_len, per_batch, *refs):
    n_slab = D_MODEL // LANES
    x_refs = refs[:n_slab]
    (mod_ref, wi_ref, bi_ref, cw_ref, cb_ref, wo_ref, bo_ref, g_ref, beta_ref,
     o_ref, acc_ref, h_ref, z_ref, slab_ref) = refs[n_slab:]
    m = _mod_row(mod_ref, per_batch)
    for p in range(PHASES):
        h_ref[p * PHASE_ROWS:(p + 1) * PHASE_ROWS, :] = _modulate(_read_phase(x_refs, p), m, 3).astype(BF16)
    seq_start, seq_end = _phase_bounds(seq_len)
    n_c = D_FF // FF_CHUNK

    def cols(c, part):
        return pl.ds(pl.multiple_of(part * D_FF + c * FF_CHUNK, FF_CHUNK), FF_CHUNK)

    def first_matmul(c, slot):
        for part in range(2):
            z_ref[slot, part] = jnp.dot(h_ref[...], wi_ref[:, cols(c, part)], preferred_element_type=F32)

    def gate_and_second_matmul(c, slot, init=False):
        conv = [_phase_conv3(z_ref[slot, part], cw_ref[:, cols(c, part)], bi_ref[:, cols(c, part)],
                             cb_ref[:, cols(c, part)], seq_start, seq_end, scale=(1.0, 0.5)[part])
                for part in range(2)]
        a = []
        for g, half_v in zip(*conv):
            t = jnp.tanh(g * (GELU_K + (GELU_K * 0.044715) * (g * g)))
            gv = g * half_v
            a.append((gv + gv * t).astype(BF16))
        a = jnp.concatenate(a, axis=0)
        row = pl.multiple_of(c * FF_CHUNK, FF_CHUNK)
        upd = jnp.dot(a, wo_ref[pl.ds(row, FF_CHUNK), :], preferred_element_type=F32)
        if init:
            acc_ref[...] = upd
        else:
            acc_ref[...] += upd

    first_matmul(0, 0)
    first_matmul(1, 1)
    gate_and_second_matmul(0, 0, init=True)

    def pair(i, carry):
        c = 2 * i + 1
        first_matmul(c + 1, 0)
        gate_and_second_matmul(c, 1)
        first_matmul(c + 2, 1)
        gate_and_second_matmul(c + 1, 0)
        return carry

    lax.fori_loop(0, (n_c - 3) // 2, pair, 0)
    first_matmul(n_c - 1, 0)
    gate_and_second_matmul(n_c - 2, 1)
    gate_and_second_matmul(n_c - 1, 0)
    for p in range(PHASES):
        out = acc_ref[p * PHASE_ROWS:(p + 1) * PHASE_ROWS, :] + bo_ref[...]
        y = _residual_layer_norm(_read_phase(x_refs, p), out, _mod(m, 5), g_ref[...], beta_ref[...])
        _slab_write_phase(slab_ref, p, y)
    _slab_drain(slab_ref, o_ref)


def _ffn(x, mods, W, layer, seq_len, per_batch):
    t = x.shape[0]
    assert (D_FF // FF_CHUNK) % 2 == 1
    row = pl.BlockSpec((ROW_TILE, D_MODEL), lambda i: (i, 0))
    names = ["ff_in_w", "ff_in_b", "ff_conv_w", "ff_conv_b", "ff_out_w", "ff_out_b"]
    return pl.pallas_call(
        functools.partial(_ffn_kernel, seq_len, per_batch),
        grid=(t // ROW_TILE,),
        in_specs=_column_slab_specs() + [_mods_spec(layer, per_batch)] + [_layer_block(W[n], layer) for n in names]
                 + [_layer_block(W["ln_g"], layer, 1), _layer_block(W["ln_b"], layer, 1)],
        out_specs=row,
        out_shape=jax.ShapeDtypeStruct((t, D_MODEL), F32),
        scratch_shapes=[pltpu.VMEM((ROW_TILE, D_MODEL), F32), pltpu.VMEM((ROW_TILE, D_MODEL), BF16),
                        pltpu.VMEM((2, 2, ROW_TILE, FF_CHUNK), F32), _slab_scratch()],
        compiler_params=_params(1),
        name="conv_ffn",
    )(*([x] * (D_MODEL // LANES)), mods, *[W[n] for n in names], W["ln_g"], W["ln_b"])


def _qkv_kernel(rope, per_batch, x_ref, mod_ref, w_ref, b_ref, qg_ref, kg_ref, *rest):
    scale = LOG2_E * HEAD_DIM ** -0.5
    h = _modulate(x_ref[...], _mod_row(mod_ref, per_batch), 0).astype(BF16)
    z = jnp.dot(h, w_ref[...], preferred_element_type=F32) + b_ref[...]
    if rope:
        ws_ref, bs_ref, qgs_ref, kgs_ref, cos_ref, sin_ref, q_ref, k_ref, v_ref = rest
        zp = jnp.dot(h, ws_ref[...], preferred_element_type=F32) + bs_ref[...]
        cos, sin = cos_ref[...], sin_ref[...]
        tables = {"q": (qg_ref[...] * scale * cos, qgs_ref[...] * scale * sin),
                  "k": (kg_ref[...] * cos, kgs_ref[...] * sin)}
    else:
        q_ref, k_ref, v_ref = rest

    def head(col, kind):
        zi = z[:, col:col + HEAD_DIM]
        inv = lax.rsqrt(jnp.mean(zi * zi, axis=-1, keepdims=True) + QK_EPS)
        if rope:
            gc, gs = tables[kind]
            return inv * (zi * gc + zp[:, col:col + HEAD_DIM] * gs)
        if kind == "q":
            return zi * inv * (qg_ref[...] * scale)
        return zi * inv * kg_ref[...]

    for i in range(N_HEADS):
        q_ref[:, i * HEAD_DIM:(i + 1) * HEAD_DIM] = head(i * HEAD_DIM, "q").astype(q_ref.dtype)
    for i in range(N_KV_HEADS):
        cols = slice(i * HEAD_DIM, (i + 1) * HEAD_DIM)
        k_head = head(Q_DIM + i * HEAD_DIM, "k")
        v_head = z[:, QK_DIM + i * HEAD_DIM:QK_DIM + (i + 1) * HEAD_DIM]
        if k_ref.shape[1] == HEAD_DIM:
            rows = pl.ds(i, ROW_TILE, stride=N_KV_HEADS)
            k_ref[rows, :] = k_head.astype(k_ref.dtype)
            v_ref[rows, :] = v_head.astype(v_ref.dtype)
        else:
            k_ref[:, cols] = k_head.astype(k_ref.dtype)
            v_ref[:, cols] = v_head.astype(v_ref.dtype)


def _qkv(x, mods, W, j, layer, rope_tables, kv_dtype, per_batch):
    t = x.shape[0]
    rope = rope_tables is not None
    row = pl.BlockSpec((ROW_TILE, D_MODEL), lambda i: (i, 0))
    if kv_dtype == F32:
        kv_shape = (t * N_KV_HEADS, HEAD_DIM)
        kv_row = pl.BlockSpec((ROW_TILE * N_KV_HEADS, HEAD_DIM), lambda i: (i, 0))
    else:
        kv_shape = (t, KV_DIM)
        kv_row = pl.BlockSpec((ROW_TILE, KV_DIM), lambda i: (i, 0))
    names = ["at_qkv_w", "at_qkv_b", "at_q_gain", "at_k_gain"]
    if rope:
        names += ["at_qk_w_swap", "at_qk_b_swap", "at_q_gain_swap", "at_k_gain_swap"]
    in_specs = [row, _mods_spec(layer, per_batch)] + [_layer_block(W[n], j) for n in names]
    args = [x, mods] + [W[n] for n in names]
    if rope:
        in_specs += [_resident((ROW_TILE, HEAD_DIM))] * 2
        args += list(rope_tables)
    return pl.pallas_call(
        functools.partial(_qkv_kernel, rope, per_batch),
        grid=(t // ROW_TILE,),
        in_specs=in_specs,
        out_specs=[row, kv_row, kv_row],
        out_shape=[jax.ShapeDtypeStruct((t, Q_DIM), BF16), jax.ShapeDtypeStruct(kv_shape, kv_dtype),
                   jax.ShapeDtypeStruct(kv_shape, kv_dtype)],
        compiler_params=_params(1),
        name="qkv_rope" if rope else "qkv",
    )(*args)


def _attn_kernel(has_ctx, seq_len, q_ref, k_ref, v_ref, *rest):
    if has_ctx:
        kc_ref, vc_ref, o_ref = rest
        kc = kc_ref[...]
        vc = jnp.concatenate([vc_ref[...], jnp.ones(vc_ref.shape, BF16)], axis=1)
    else:
        (o_ref,) = rest
    nt = (((1,), (1,)), ((), ()))
    heads_in_block = N_KV_HEADS if k_ref.dtype == F32 else 1
    n_seq = k_ref.shape[0] // (seq_len * heads_in_block)
    q_rows = q_ref.shape[0] // n_seq
    ones = jnp.ones((seq_len, HEAD_DIM), BF16)
    for s in range(n_seq):
        if heads_in_block > 1:
            kv_rows = pl.ds(s * seq_len * heads_in_block + pl.program_id(1), seq_len, stride=heads_in_block)
        else:
            kv_rows = slice(s * seq_len, (s + 1) * seq_len)
        k = k_ref[kv_rows, :].astype(BF16)
        v = jnp.concatenate([v_ref[kv_rows, :].astype(BF16), ones], axis=1)
        rows = slice(s * q_rows, (s + 1) * q_rows)
        for g0 in range(0, GROUP, HEAD_STACK):
            heads = range(g0, g0 + HEAD_STACK)
            q = jnp.concatenate([q_ref[rows, g * HEAD_DIM:(g + 1) * HEAD_DIM] for g in heads], axis=0)
            sc_new = lax.dot_general(q, k, nt, preferred_element_type=F32)
            mx = jnp.max(sc_new, axis=-1, keepdims=True)
            if has_ctx:
                sc_ctx = lax.dot_general(q, kc, nt, preferred_element_type=F32)
                mx = jnp.maximum(mx, jnp.max(sc_ctx, axis=-1, keepdims=True))
            o = jnp.dot(jnp.exp2(sc_new - mx).astype(BF16), v, preferred_element_type=F32)
            if has_ctx:
                o = o + jnp.dot(jnp.exp2(sc_ctx - mx).astype(BF16), vc, preferred_element_type=F32)
            o = (o[:, :HEAD_DIM] / o[:, HEAD_DIM:]).astype(BF16)
            for n, g in enumerate(heads):
                o_ref[rows, g * HEAD_DIM:(g + 1) * HEAD_DIM] = o[n * q_rows:(n + 1) * q_rows]


def _attention(q, k, v, ctx, seq_len, q_tile):
    t = q.shape[0]
    gw = GROUP * HEAD_DIM
    kv_rows = max(seq_len, q_tile)
    n_b = t // kv_rows
    n_q = kv_rows // q_tile
    q_spec = pl.BlockSpec((q_tile, gw), lambda b, h, i: (b * n_q + i, h))
    if k.dtype == F32:
        kv_spec = pl.BlockSpec((kv_rows * N_KV_HEADS, HEAD_DIM), lambda b, h, i: (b, 0))
    else:
        kv_spec = pl.BlockSpec((kv_rows, HEAD_DIM), lambda b, h, i: (b, h))
    in_specs = [q_spec, kv_spec, kv_spec]
    args = [q, k, v]
    if ctx is not None:
        kc, vc, layer = ctx
        ctx_spec = pl.BlockSpec((None, None, kc.shape[2], HEAD_DIM), lambda b, h, i: (b, layer, 0, h))
        in_specs += [ctx_spec, ctx_spec]
        args += [kc, vc]
    return pl.pallas_call(
        functools.partial(_attn_kernel, ctx is not None, seq_len),
        grid=(n_b, N_KV_HEADS, n_q),
        in_specs=in_specs,
        out_specs=q_spec,
        out_shape=jax.ShapeDtypeStruct((t, Q_DIM), BF16),
        compiler_params=_params(3),
        name="attention_ctx" if ctx is not None else "attention",
    )(*args)


def _hyena_feats(seq_len):
    t = np.linspace(0.0, 1.0, seq_len)[:, None]
    n_bands = (POS_EMB_DIM - 1) // 2
    w = 2.0 * math.pi * np.arange(seq_len) / seq_len
    f = np.linspace(1e-4, n_bands - 1, n_bands)
    ang = w[:, None] * f[None, :]
    z = np.concatenate([t, np.cos(ang), -np.sin(ang)], -1)
    z = np.pad(z, ((0, 0), (0, FILTER_PAD - POS_EMB_DIM)))
    max_decay = math.log(DECAY_TARGET) / FAST_DECAY_PCT
    min_decay = math.log(DECAY_TARGET) / SLOW_DECAY_PCT
    deltas = np.abs(np.linspace(min_decay, max_decay, D_MODEL))[None, :]
    return z.astype(np.float32), t.astype(np.float32), deltas.astype(np.float32)


def _dft_mats(seq_len):
    n = 2 * seq_len
    idx = np.arange(seq_len)
    ang = 2.0 * math.pi * ((idx[:, None] * idx[None, :]) % n) / n
    sign = np.where(idx % 2 == 0, 1.0, -1.0)
    fa, fb = np.cos(ang), -np.sin(ang)
    fb[0, :] = sign
    fm = np.concatenate([fa, fb], 0)
    ga, gb = (2.0 / n) * np.cos(ang), -(2.0 / n) * np.sin(ang)
    ga[:, 0] = 1.0 / n
    gb[:, 0] = sign / n
    time_of = (np.arange(seq_len) % (seq_len // PHASES)) * PHASES + np.arange(seq_len) // (seq_len // PHASES)
    return (fm.astype(np.float32), fm[:, time_of].astype(np.float32),
            ga[time_of, :].astype(np.float32), gb[time_of, :].astype(np.float32))


def _rope_tables(seq_len):
    rows = np.repeat(np.arange(seq_len // GRID_W), GRID_W).astype(np.float64)
    cols = np.tile(np.arange(GRID_W), seq_len // GRID_W).astype(np.float64)
    half = HEAD_DIM // 2
    inv = ROPE_THETA ** (-np.arange(0, half, 2, dtype=np.float64) / half)
    ang = np.concatenate([rows[:, None] * inv, cols[:, None] * inv], -1)
    cos = np.repeat(np.cos(ang), 2, axis=-1)
    sin = np.stack([-np.sin(ang), np.sin(ang)], axis=-1).reshape(seq_len, HEAD_DIM)
    return cos.astype(np.float32), sin.astype(np.float32)


def _pair_swap_matrix(n):
    p = np.zeros((n, n), np.float32)
    idx = np.arange(n)
    p[idx ^ 1, idx] = 1.0
    return p


def _trunk(x, mods, seq_len, per_batch, W, rope, ctx):
    fm, fm_pm, ga_pm, gb_pm = (jnp.asarray(a).astype(BF16) for a in _dft_mats(seq_len))
    feats, t_col, deltas = (jnp.asarray(a) for a in _hyena_feats(seq_len))
    new_k, new_v = [], []
    for l in range(DEPTH):
        j = l // 2
        if l % 2 == 0:
            ha, hb, hc = _hyena_filter((feats, t_col, deltas, fm), W, j, seq_len)
            x0, u = _hyena_in(x, mods, W, j, l, seq_len, per_batch)
            a = _dft_conv(u, x0, ha, hb, hc, fm_pm, ga_pm, gb_pm, seq_len)
            w_o, b_o = W["hy_out_w"], W["hy_out_b"]
        else:
            if ctx is None:
                q, k, v = _qkv(x, mods, W, j, l, None, F32, per_batch)
                new_k.append(k)
                new_v.append(v)
                a = _attention(q, k, v, None, seq_len, ROW_TILE)
            else:
                q, k, v = _qkv(x, mods, W, j, l, rope, BF16, per_batch)
                a = _attention(q, k, v, (ctx[0], ctx[1], j), seq_len, CTX_Q_TILE)
            w_o, b_o = W["at_o_w"], W["at_o_b"]
        x = _proj_ln(a, x, mods, w_o, b_o, j, W, l, per_batch, l % 2 == 0)
        x = _ffn(x, mods, W, l, seq_len, per_batch)
    return x, new_k, new_v


def kernel(x_prompt, x_sample, cache_k, cache_v, c, c_ctx, w_mod, b_mod, ln_g, ln_b, hy_in_w, hy_in_b, hy_short_w, hy_short_b, hy_pos_w1, hy_pos_b1, hy_pos_w2, hy_pos_b2, hy_pos_wout, hy_freq, hy_filt_bias, hy_out_w, hy_out_b, at_qkv_w, at_qkv_b, at_q_gain, at_k_gain, at_o_w, at_o_b, ff_in_w, ff_in_b, ff_conv_w, ff_conv_b, ff_out_w, ff_out_b):
    batch, seq, _ = x_prompt.shape
    dec_batch, dec_seq, _ = x_sample.shape
    assert dec_seq == ROW_TILE and ROW_TILE % seq == 0 and 1 + dec_batch <= COND_ROWS
    n_attn = at_qkv_w.shape[0]
    pad = FILTER_PAD - FILTER_WIDTH

    def vec(a):
        return a[:, None, :]

    qkv_w = at_qkv_w.astype(BF16)
    swap = jnp.asarray(_pair_swap_matrix(QK_DIM))
    swap_head = jnp.asarray(_pair_swap_matrix(HEAD_DIM))
    hi = lax.Precision.HIGHEST
    W = {
        "ln_g": ln_g[:, :, None, :], "ln_b": ln_b[:, :, None, :],
        "hy_in_w": hy_in_w.astype(BF16), "hy_in_b": vec(hy_in_b),
        "hy_short_w": hy_short_w, "hy_short_b": vec(hy_short_b),
        "hy_pos_w1": jnp.pad(hy_pos_w1, ((0, 0), (0, FILTER_PAD - POS_EMB_DIM), (0, pad))),
        "hy_pos_b1": vec(jnp.pad(hy_pos_b1, ((0, 0), (0, pad)))),
        "hy_pos_w2": jnp.pad(hy_pos_w2, ((0, 0), (0, 0), (0, pad), (0, pad))),
        "hy_pos_b2": jnp.pad(hy_pos_b2, ((0, 0), (0, 0), (0, pad)))[:, :, None, :],
        "hy_freq": vec(jnp.pad(hy_freq, ((0, 0), (0, pad)))),
        "hy_pos_wout": hy_pos_wout, "hy_filt_bias": vec(hy_filt_bias),
        "hy_out_w": hy_out_w.astype(BF16), "hy_out_b": vec(hy_out_b),
        "at_qkv_w": qkv_w, "at_qkv_b": vec(at_qkv_b),
        "at_q_gain": vec(at_q_gain), "at_k_gain": vec(at_k_gain),
        "at_qk_w_swap": jnp.dot(qkv_w[:, :, :QK_DIM], swap.astype(BF16), preferred_element_type=F32).astype(BF16),
        "at_qk_b_swap": vec(jnp.dot(at_qkv_b[:, :QK_DIM], swap, precision=hi)),
        "at_q_gain_swap": vec(jnp.dot(at_q_gain, swap_head, precision=hi)),
        "at_k_gain_swap": vec(jnp.dot(at_k_gain, swap_head, precision=hi)),
        "at_o_w": at_o_w.astype(BF16), "at_o_b": vec(at_o_b),
        "ff_in_w": ff_in_w.astype(BF16), "ff_in_b": vec(ff_in_b),
        "ff_conv_w": ff_conv_w, "ff_conv_b": vec(ff_conv_b),
        "ff_out_w": ff_out_w.astype(BF16), "ff_out_b": vec(ff_out_b),
    }
    rope = tuple(jnp.asarray(a) for a in _rope_tables(dec_seq))

    cond = jnp.concatenate([c_ctx[None, :], c, jnp.zeros((COND_ROWS - 1 - dec_batch, D_MODEL), F32)], 0)
    mods = _modulation_all(cond, w_mod, b_mod)

    past = cache_k.shape[2]
    ctx_k = cache_k.astype(BF16).reshape(dec_batch, n_attn, past, KV_DIM)
    ctx_v = cache_v.astype(BF16).reshape(dec_batch, n_attn, past, KV_DIM)

    y_p, new_k, new_v = _trunk(x_prompt.reshape(batch * seq, D_MODEL), mods, seq, False, W, None, None)
    y_s, _, _ = _trunk(x_sample.reshape(dec_batch * dec_seq, D_MODEL), mods, dec_seq, True, W, rope, (ctx_k, ctx_v))

    kv_shape = (batch, seq, N_KV_HEADS, HEAD_DIM)
    new_cache_k = jnp.stack([k.reshape(kv_shape) for k in new_k], axis=1)
    new_cache_v = jnp.stack([v.reshape(kv_shape) for v in new_v], axis=1)
    return (y_p.reshape(batch, seq, D_MODEL), y_s.reshape(dec_batch, dec_seq, D_MODEL), new_cache_k, new_cache_v)
```
